```python
import jax, jax.numpy as jnp
from jax import lax
import numpy as np

D_MODEL = 1024
BATCH = 8
SEQ = 4096
DEPTH = 2

CONV_WIDTH = 512
CONV_K = 3
MLA_HEADS = 8
MLA_NOPE = 64
MLA_ROPE = 32
MLA_V = 64
MLA_Q_RANK = 256
MLA_KV_RANK = 128
ROPE_THETA = 10000.0
SB_HEADS = 8
SB_HEAD_DIM = 64
SB_WIDTH = SB_HEADS * SB_HEAD_DIM
Q_BLOCK = 128
N_BRANCHES = 3
OFF_CONV = 0
OFF_CQ = OFF_CONV + 3 * CONV_WIDTH
OFF_CKV = OFF_CQ + MLA_Q_RANK
OFF_KR = OFF_CKV + MLA_KV_RANK
OFF_SB = OFF_KR + MLA_ROPE
OFF_GATE = OFF_SB + 3 * SB_WIDTH
IN_COLS = OFF_GATE + N_BRANCHES * D_MODEL
N_GROUPS = 4
EXPERTS_PER_GROUP = 8
N_EXPERTS = N_GROUPS * EXPERTS_PER_GROUP
TOP_K_IN_GROUP = 2
D_EXPERT = 256
EPS = 1e-6

kernel_name = "hybrid_conv_mla_stickbreak_hiermoe"


def rms_norm(x, g):
    xf = x.astype(jnp.float32)
    y = xf * lax.rsqrt(jnp.mean(xf * xf, axis=-1, keepdims=True) + EPS)
    return (y * g.astype(jnp.float32)).astype(x.dtype)


def apply_rope(x, positions):
    half = x.shape[-1] // 2
    freqs = ROPE_THETA ** (-jnp.arange(half, dtype=jnp.float32) / half)
    ang = positions.astype(jnp.float32)[..., None] * freqs
    ang = ang.reshape(ang.shape[:2] + (1,) * (x.ndim - 3) + (half,))
    cos, sin = jnp.cos(ang), jnp.sin(ang)
    xf = x.astype(jnp.float32)
    x1, x2 = xf[..., :half], xf[..., half:]
    out = jnp.concatenate([x1 * cos - x2 * sin, x2 * cos + x1 * sin], axis=-1)
    return out.astype(x.dtype)


def short_conv_mixer(h_c, b_g, c_g, conv_w, w_out):
    u = c_g * h_c
    y = lax.conv_general_dilated(
        u, conv_w[:, None, :].astype(u.dtype), window_strides=(1,),
        padding=[(CONV_K - 1, 0)], dimension_numbers=("NWC", "WIO", "NWC"),
        feature_group_count=CONV_WIDTH)
    return (b_g * y) @ w_out


def mla_mixer(c_q, c_kv, k_rope, positions, q_norm, kv_norm, w_uq, w_ukv, w_out):
    B, S, _ = c_q.shape
    q = (rms_norm(c_q, q_norm) @ w_uq).reshape(B, S, MLA_HEADS, MLA_NOPE + MLA_ROPE)
    q_nope = q[..., :MLA_NOPE]
    q_rope = apply_rope(q[..., MLA_NOPE:], positions)
    kv = (rms_norm(c_kv, kv_norm) @ w_ukv).reshape(B, S, MLA_HEADS, MLA_NOPE + MLA_V)
    k_nope, v = kv[..., :MLA_NOPE], kv[..., MLA_NOPE:]
    k_rope = apply_rope(k_rope, positions)
    scale = (MLA_NOPE + MLA_ROPE) ** -0.5
    outs = []
    for i in range(S // Q_BLOCK):
        t0, kend = i * Q_BLOCK, (i + 1) * Q_BLOCK
        s = jnp.einsum("bqhd,bkhd->bhqk", q_nope[:, t0:kend], k_nope[:, :kend],
                       preferred_element_type=jnp.float32)
        s = s + jnp.einsum("bqhr,bkr->bhqk", q_rope[:, t0:kend], k_rope[:, :kend],
                           preferred_element_type=jnp.float32)
        mask = (t0 + jnp.arange(Q_BLOCK))[:, None] >= jnp.arange(kend)[None, :]
        s = jnp.where(mask, s * scale, -jnp.inf)
        p = jax.nn.softmax(s, axis=-1).astype(v.dtype)
        outs.append(jnp.einsum("bhqk,bkhd->bqhd", p, v[:, :kend]))
    o = jnp.concatenate(outs, axis=1).reshape(B, S, MLA_HEADS * MLA_V)
    return o @ w_out


def stick_breaking_mixer(q, k, v, w_out):
    B, S, _ = q.shape
    q = q.reshape(B, S, SB_HEADS, SB_HEAD_DIM)
    k = k.reshape(B, S, SB_HEADS, SB_HEAD_DIM)
    v = v.reshape(B, S, SB_HEADS, SB_HEAD_DIM)
    scale = SB_HEAD_DIM ** -0.5
    outs = []
    for i in range(S // Q_BLOCK):
        t0, kend = i * Q_BLOCK, (i + 1) * Q_BLOCK
        z = jnp.einsum("bqhd,bkhd->bhqk", q[:, t0:kend], k[:, :kend],
                       preferred_element_type=jnp.float32) * scale
        mask = (t0 + jnp.arange(Q_BLOCK))[:, None] > jnp.arange(kend)[None, :]
        log_1m = jnp.where(mask, jax.nn.log_sigmoid(-z), 0.0)
        suffix = lax.cumsum(log_1m, axis=3, reverse=True) - log_1m
        a = jnp.where(mask, jnp.exp(jax.nn.log_sigmoid(z) + suffix), 0.0)
        outs.append(jnp.einsum("bhqk,bkhd->bqhd", a.astype(v.dtype), v[:, :kend]))
    o = jnp.concatenate(outs, axis=1).reshape(B, S, SB_WIDTH)
    return o @ w_out


def hierarchical_moe(h, w_rg, b_rg, w_re, b_re, w_g, w_u, w_d):
    B, S, D = h.shape
    hf = h.reshape(B * S, D)
    g_prob = jax.nn.softmax((hf @ w_rg).astype(jnp.float32) + b_rg.astype(jnp.float32), axis=-1)
    g_w, g_idx = lax.top_k(g_prob, 1)
    e_logits = ((hf @ w_re).astype(jnp.float32) + b_re.astype(jnp.float32)).reshape(
        B * S, N_GROUPS, EXPERTS_PER_GROUP)
    e_sel = jnp.einsum("nge,ng->ne", e_logits,
                       jax.nn.one_hot(g_idx[:, 0], N_GROUPS, dtype=jnp.float32))
    e_w, e_idx = lax.top_k(jax.nn.softmax(e_sel, axis=-1), TOP_K_IN_GROUP)
    e_w = e_w / jnp.sum(e_w, axis=-1, keepdims=True)
    weights = g_w * e_w
    expert_id = g_idx * EXPERTS_PER_GROUP + e_idx
    combine = jnp.sum(jax.nn.one_hot(expert_id, N_EXPERTS, dtype=jnp.float32)
                      * weights[..., None], axis=1).astype(hf.dtype)
    out = jnp.zeros_like(hf)
    for e in range(N_EXPERTS):
        act = jax.nn.silu(hf @ w_g[e]) * (hf @ w_u[e])
        out = out + combine[:, e:e + 1] * (act @ w_d[e])
    return out.reshape(B, S, D)


def hybrid_layer(x, positions, attn_norm, w_in, b_gate, conv_w, w_out_conv, q_norm, kv_norm,
                 w_uq, w_ukv, w_out_mla, w_out_sb, w_o, ffn_norm, w_router_group,
                 b_router_group, w_router_expert, b_router_expert, w_exp_gate, w_exp_up,
                 w_exp_down):
    B, S, D = x.shape
    h = rms_norm(x, attn_norm)
    proj = h @ w_in
    h_c = proj[..., OFF_CONV:OFF_CONV + CONV_WIDTH]
    b_g = proj[..., OFF_CONV + CONV_WIDTH:OFF_CONV + 2 * CONV_WIDTH]
    c_g = proj[..., OFF_CONV + 2 * CONV_WIDTH:OFF_CQ]
    c_q = proj[..., OFF_CQ:OFF_CKV]
    c_kv = proj[..., OFF_CKV:OFF_KR]
    k_r = proj[..., OFF_KR:OFF_SB]
    q_sb = proj[..., OFF_SB:OFF_SB + SB_WIDTH]
    k_sb = proj[..., OFF_SB + SB_WIDTH:OFF_SB + 2 * SB_WIDTH]
    v_sb = proj[..., OFF_SB + 2 * SB_WIDTH:OFF_GATE]
    gates = jax.nn.sigmoid((proj[..., OFF_GATE:] + b_gate).astype(jnp.float32))
    gates = gates.astype(x.dtype).reshape(B, S, N_BRANCHES, D)

    y_a = short_conv_mixer(h_c, b_g, c_g, conv_w, w_out_conv)
    y_b = mla_mixer(c_q, c_kv, k_r, positions, q_norm, kv_norm, w_uq, w_ukv, w_out_mla)
    y_c = stick_breaking_mixer(q_sb, k_sb, v_sb, w_out_sb)
    merged = gates[:, :, 0] * y_a + gates[:, :, 1] * y_b + gates[:, :, 2] * y_c
    x = x + merged @ w_o

    h2 = rms_norm(x, ffn_norm)
    x = x + hierarchical_moe(h2, w_router_group, b_router_group, w_router_expert,
                             b_router_expert, w_exp_gate, w_exp_up, w_exp_down)
    return x


def setup_inputs(seed: int = 0) -> dict:
    key = jax.random.key(seed)
    ks = jax.random.split(key, 24)
    f32 = jnp.float32

    def nrm(k, shape, fan_in):
        return jax.random.normal(k, shape, f32) * (fan_in ** -0.5)

    def gain(k, shape):
        return 1.0 + 0.02 * jax.random.normal(k, shape, f32)

    x = jax.random.normal(ks[0], (BATCH, SEQ, D_MODEL), f32)
    offsets = jax.random.randint(ks[1], (BATCH, 1), 0, 4096, dtype=jnp.int32)
    positions = offsets + jnp.arange(SEQ, dtype=jnp.int32)[None, :]
    return {
        "x": x,
        "positions": positions,
        "attn_norm": gain(ks[2], (DEPTH, D_MODEL)),
        "w_in": nrm(ks[3], (DEPTH, D_MODEL, IN_COLS), D_MODEL),
        "b_gate": 0.02 * jax.random.normal(ks[4], (DEPTH, N_BRANCHES * D_MODEL), f32),
        "conv_w": nrm(ks[5], (DEPTH, CONV_K, CONV_WIDTH), CONV_K),
        "w_out_conv": nrm(ks[6], (DEPTH, CONV_WIDTH, D_MODEL), CONV_WIDTH),
        "q_norm": gain(ks[7], (DEPTH, MLA_Q_RANK)),
        "kv_norm": gain(ks[8], (DEPTH, MLA_KV_RANK)),
        "w_uq": nrm(ks[9], (DEPTH, MLA_Q_RANK, MLA_HEADS * (MLA_NOPE + MLA_ROPE)), MLA_Q_RANK),
        "w_ukv": nrm(ks[10], (DEPTH, MLA_KV_RANK, MLA_HEADS * (MLA_NOPE + MLA_V)), MLA_KV_RANK),
        "w_out_mla": nrm(ks[11], (DEPTH, MLA_HEADS * MLA_V, D_MODEL), MLA_HEADS * MLA_V),
        "w_out_sb": nrm(ks[12], (DEPTH, SB_WIDTH, D_MODEL), SB_WIDTH),
        "w_o": nrm(ks[13], (DEPTH, D_MODEL, D_MODEL), D_MODEL),
        "ffn_norm": gain(ks[14], (DEPTH, D_MODEL)),
        "w_router_group": nrm(ks[15], (DEPTH, D_MODEL, N_GROUPS), D_MODEL),
        "b_router_group": 0.01 * jax.random.normal(ks[16], (DEPTH, N_GROUPS), f32),
        "w_router_expert": nrm(ks[17], (DEPTH, D_MODEL, N_EXPERTS), D_MODEL),
        "b_router_expert": 0.01 * jax.random.normal(ks[18], (DEPTH, N_EXPERTS), f32),
        "w_exp_gate": nrm(ks[19], (DEPTH, N_EXPERTS, D_MODEL, D_EXPERT), D_MODEL),
        "w_exp_up": nrm(ks[20], (DEPTH, N_EXPERTS, D_MODEL, D_EXPERT), D_MODEL),
        "w_exp_down": nrm(ks[21], (DEPTH, N_EXPERTS, D_EXPERT, D_MODEL), D_EXPERT),
        "final_norm": gain(ks[22], (D_MODEL,)),
    }


def reference(x, positions, attn_norm, w_in, b_gate, conv_w, w_out_conv, q_norm, kv_norm,
              w_uq, w_ukv, w_out_mla, w_out_sb, w_o, ffn_norm, w_router_group,
              b_router_group, w_router_expert, b_router_expert, w_exp_gate, w_exp_up,
              w_exp_down, final_norm):
    for l in range(DEPTH):
        x = hybrid_layer(x, positions, attn_norm[l], w_in[l], b_gate[l], conv_w[l],
                         w_out_conv[l], q_norm[l], kv_norm[l], w_uq[l], w_ukv[l],
                         w_out_mla[l], w_out_sb[l], w_o[l], ffn_norm[l], w_router_group[l],
                         b_router_group[l], w_router_expert[l], b_router_expert[l],
                         w_exp_gate[l], w_exp_up[l], w_exp_down[l])
    return rms_norm(x, final_norm)
```

```python
import functools

import jax
import jax.numpy as jnp
from jax import lax
from jax.experimental import pallas as pl
from jax.experimental.pallas import tpu as pltpu

D_MODEL = 1024
CONV_WIDTH = 512
CONV_K = 3
MLA_HEADS = 8
MLA_NOPE = 64
MLA_ROPE = 32
MLA_V = 64
MLA_Q_RANK = 256
MLA_KV_RANK = 128
ROPE_THETA = 10000.0
SB_HEADS = 8
SB_HEAD_DIM = 64
SB_WIDTH = SB_HEADS * SB_HEAD_DIM
N_BRANCHES = 3
OFF_CONV = 0
OFF_CQ = OFF_CONV + 3 * CONV_WIDTH
OFF_CKV = OFF_CQ + MLA_Q_RANK
OFF_KR = OFF_CKV + MLA_KV_RANK
OFF_SB = OFF_KR + MLA_ROPE
OFF_GATE = OFF_SB + 3 * SB_WIDTH
N_GROUPS = 4
EXPERTS_PER_GROUP = 8
N_EXPERTS = N_GROUPS * EXPERTS_PER_GROUP
D_EXPERT = 256
EPS = 1e-6

LANES = 128
HEAD_PAD = 128
ROPE_HALF = MLA_ROPE // 2
GATE_COLS = N_BRANCHES * D_MODEL
CONV_COLS = 3 * CONV_WIDTH
LAT_COLS = MLA_Q_RANK + MLA_KV_RANK + 2 * HEAD_PAD
SB_COLS = 3 * SB_WIDTH
ROUTER_COLS = LANES
VMEM_LIMIT = 56 * 1024 * 1024

BF16 = jnp.bfloat16
F32 = jnp.float32


def _params(n_axes, vmem=VMEM_LIMIT):
    return pltpu.CompilerParams(dimension_semantics=("arbitrary",) * n_axes, vmem_limit_bytes=vmem)


def _rms(xf, gain):
    return xf * lax.rsqrt(jnp.mean(xf * xf, axis=-1, keepdims=True) + EPS) * gain


def _const_spec(shape):
    return pl.BlockSpec(shape, lambda *_: (0,) * len(shape))


def _rope_table_kernel(pos_ref, freq_ref, cos_ref, sin_ref):
    ang = pos_ref[...].astype(F32) * freq_ref[...]
    cos_ref[...] = jnp.cos(ang)
    sin_ref[...] = jnp.sin(ang)


def _rope_tables(positions, tm):
    n = positions.size
    half = ROPE_HALF
    freqs = ROPE_THETA ** (-jnp.arange(half, dtype=F32) / half)
    zeros = jnp.zeros((MLA_NOPE,), F32)
    freq_row = jnp.concatenate([zeros, freqs, freqs, jnp.zeros((HEAD_PAD - MLA_NOPE - MLA_ROPE,), F32)])[None, :]
    pos = positions.reshape(n, 1)
    return pl.pallas_call(
        _rope_table_kernel,
        grid=(n // tm,),
        in_specs=[pl.BlockSpec((tm, 1), lambda i: (i, 0)), _const_spec((1, HEAD_PAD))],
        out_specs=[pl.BlockSpec((tm, HEAD_PAD), lambda i: (i, 0))] * 2,
        out_shape=[jax.ShapeDtypeStruct((n, HEAD_PAD), F32)] * 2,
        compiler_params=_params(1),
        name="rope_tables",
    )(pos, freq_row)


def _in_proj_kernel(x_ref, g_ref, w_ref, b_ref, gate_ref, conv_ref, lat_ref, sb_ref, *, chunk):
    h = _rms(x_ref[...], g_ref[...]).astype(BF16)

    def run(out_ref, col0, width, epilogue):
        for c in range(0, width, chunk):
            cw = min(chunk, width - c)
            acc = jnp.dot(h, w_ref[:, col0 + c:col0 + c + cw], preferred_element_type=F32)
            out_ref[:, c:c + cw] = epilogue(acc, c, cw).astype(out_ref.dtype)

    run(gate_ref, 0, GATE_COLS, lambda a, c, cw: jax.nn.sigmoid(a + b_ref[:, c:c + cw]))
    ident = lambda a, c, cw: a
    run(conv_ref, GATE_COLS, CONV_COLS, ident)
    run(lat_ref, GATE_COLS + CONV_COLS, LAT_COLS, ident)
    run(sb_ref, GATE_COLS + CONV_COLS + LAT_COLS, SB_COLS, ident)


def _in_proj(x2, gain, w_packed, b_gate, tm):
    n = x2.shape[0]
    total = w_packed.shape[1]
    widths = (GATE_COLS, CONV_COLS, LAT_COLS, SB_COLS)
    return pl.pallas_call(
        functools.partial(_in_proj_kernel, chunk=512),
        grid=(n // tm,),
        in_specs=[pl.BlockSpec((tm, D_MODEL), lambda i: (i, 0)), _const_spec((1, D_MODEL)),
                  _const_spec((D_MODEL, total)), _const_spec((1, GATE_COLS))],
        out_specs=[pl.BlockSpec((tm, w), lambda i: (i, 0)) for w in widths],
        out_shape=[jax.ShapeDtypeStruct((n, w), BF16) for w in widths],
        compiler_params=_params(1),
        name="in_proj",
    )(x2, gain, w_packed, b_gate)


def _pack_w_in(w):
    kr = w[:, OFF_KR:OFF_SB]
    x1, x2 = kr[:, :ROPE_HALF], kr[:, ROPE_HALF:]
    z_lo = jnp.zeros((D_MODEL, MLA_NOPE), w.dtype)
    z_hi = jnp.zeros((D_MODEL, HEAD_PAD - MLA_NOPE - MLA_ROPE), w.dtype)
    kr_main = jnp.concatenate([z_lo, x1, x2, z_hi], axis=1)
    kr_swap = jnp.concatenate([z_lo, -x2, x1, z_hi], axis=1)
    return jnp.concatenate([w[:, OFF_GATE:], w[:, OFF_CONV:OFF_CQ], w[:, OFF_CQ:OFF_KR], kr_main, kr_swap,
                            w[:, OFF_SB:OFF_GATE]], axis=1).astype(BF16)


def _mla_prep_kernel(lat_ref, cos_ref, sin_ref, qn_ref, kvn_ref, wqm_ref, wqs_ref, wk_ref, wv_ref,
                     q_ref, k_ref, v_ref, *, scale):
    cos, sin = cos_ref[...], sin_ref[...]
    cq = _rms(lat_ref[:, :MLA_Q_RANK].astype(F32), qn_ref[...]).astype(BF16)
    ckv = _rms(lat_ref[:, MLA_Q_RANK:MLA_Q_RANK + MLA_KV_RANK].astype(F32), kvn_ref[...]).astype(BF16)
    kr0 = MLA_Q_RANK + MLA_KV_RANK
    kr = (lat_ref[:, kr0:kr0 + HEAD_PAD].astype(F32) * cos
          + lat_ref[:, kr0 + HEAD_PAD:kr0 + 2 * HEAD_PAD].astype(F32) * sin)
    qm = jnp.dot(cq, wqm_ref[...], preferred_element_type=F32)
    qs = jnp.dot(cq, wqs_ref[...], preferred_element_type=F32)
    kn = jnp.dot(ckv, wk_ref[...], preferred_element_type=F32)
    for h in range(MLA_HEADS):
        sl = slice(h * HEAD_PAD, (h + 1) * HEAD_PAD)
        q_ref[:, sl] = ((qm[:, sl] * cos + qs[:, sl] * sin) * scale).astype(BF16)
        k_ref[:, sl] = (kn[:, sl] + kr).astype(BF16)
    v_ref[...] = jnp.dot(ckv, wv_ref[...], preferred_element_type=F32).astype(BF16)


def _mla_prep(lat, cos, sin, q_norm, kv_norm, wq_main, wq_swap, wk, wv, tm):
    n = lat.shape[0]
    hw = MLA_HEADS * HEAD_PAD
    row = lambda w: pl.BlockSpec((tm, w), lambda i: (i, 0))
    return pl.pallas_call(
        functools.partial(_mla_prep_kernel, scale=(MLA_NOPE + MLA_ROPE) ** -0.5),
        grid=(n // tm,),
        in_specs=[row(LAT_COLS), row(HEAD_PAD), row(HEAD_PAD), _const_spec((1, MLA_Q_RANK)),
                  _const_spec((1, MLA_KV_RANK)), _const_spec((MLA_Q_RANK, hw)), _const_spec((MLA_Q_RANK, hw)),
                  _const_spec((MLA_KV_RANK, hw)), _const_spec((MLA_KV_RANK, MLA_HEADS * MLA_V))],
        out_specs=[row(hw), row(hw), row(MLA_HEADS * MLA_V)],
        out_shape=[jax.ShapeDtypeStruct((n, hw), BF16), jax.ShapeDtypeStruct((n, hw), BF16),
                   jax.ShapeDtypeStruct((n, MLA_HEADS * MLA_V), BF16)],
        compiler_params=_params(1),
        name="mla_prep",
    )(lat, cos, sin, q_norm, kv_norm, wq_main, wq_swap, wk, wv)


def _pack_mla_weights(w_uq, w_ukv):
    qd = MLA_NOPE + MLA_ROPE
    z_hi = jnp.zeros((MLA_Q_RANK, HEAD_PAD - qd), w_uq.dtype)
    z_lo = jnp.zeros((MLA_Q_RANK, MLA_NOPE), w_uq.dtype)
    main, swap, wk, wv = [], [], [], []
    for h in range(MLA_HEADS):
        wq = w_uq[:, h * qd:(h + 1) * qd]
        nope, x1, x2 = wq[:, :MLA_NOPE], wq[:, MLA_NOPE:MLA_NOPE + ROPE_HALF], wq[:, MLA_NOPE + ROPE_HALF:]
        main += [nope, x1, x2, z_hi]
        swap += [z_lo, -x2, x1, z_hi]
        kv = w_ukv[:, h * (MLA_NOPE + MLA_V):(h + 1) * (MLA_NOPE + MLA_V)]
        wk += [kv[:, :MLA_NOPE], jnp.zeros((MLA_KV_RANK, HEAD_PAD - MLA_NOPE), w_ukv.dtype)]
        wv.append(kv[:, MLA_NOPE:])
    cat = lambda parts: jnp.concatenate(parts, axis=1).astype(BF16)
    return cat(main), cat(swap), cat(wk), cat(wv)


def _mla_attn_kernel(q_ref, k_ref, v_ref, o_ref, *, tq):
    i = pl.program_id(1)
    row = lax.broadcasted_iota(jnp.int32, (tq, tq), 0)
    col = lax.broadcasted_iota(jnp.int32, (tq, tq), 1)
    causal = row >= col
    dn = (((1,), (1,)), ((), ()))
    outs = []
    for h in range(MLA_HEADS):
        q = q_ref[:, h * HEAD_PAD:(h + 1) * HEAD_PAD]
        ksl = slice(h * HEAD_PAD, (h + 1) * HEAD_PAD)
        vsl = slice(h * MLA_V, (h + 1) * MLA_V)

        def block(j, masked):
            start = pl.multiple_of(j * tq, tq)
            s = lax.dot_general(q, k_ref[pl.ds(start, tq), ksl], dn, preferred_element_type=F32)
            if masked:
                s = jnp.where(causal, s, -jnp.inf)
            return s, v_ref[pl.ds(start, tq), vsl]

        s, v = block(i, True)
        m = jnp.max(s, axis=-1, keepdims=True)
        p = jnp.exp(s - m)
        l = jnp.sum(p, axis=-1, keepdims=True)
        acc = jnp.dot(p.astype(BF16), v, preferred_element_type=F32)

        def body(j, carry):
            m, l, acc = carry
            s, v = block(j, False)
            m_new = jnp.maximum(m, jnp.max(s, axis=-1, keepdims=True))
            alpha = jnp.exp(m - m_new)
            p = jnp.exp(s - m_new)
            l = alpha * l + jnp.sum(p, axis=-1, keepdims=True)
            acc = alpha * acc + jnp.dot(p.astype(BF16), v, preferred_element_type=F32)
            return m_new, l, acc

        m, l, acc = lax.fori_loop(0, i, body, (m, l, acc))
        outs.append(acc / l)
    o_ref[...] = jnp.concatenate(outs, axis=1).astype(o_ref.dtype)


def _mla_attn(q, k, v, batch, seq, tq):
    n = q.shape[0]
    hw = MLA_HEADS * HEAD_PAD
    vw = MLA_HEADS * MLA_V
    nq = seq // tq
    return pl.pallas_call(
        functools.partial(_mla_attn_kernel, tq=tq),
        grid=(batch, nq),
        in_specs=[pl.BlockSpec((tq, hw), lambda b, i: (b * nq + i, 0)),
                  pl.BlockSpec((seq, hw), lambda b, i: (b, 0)),
                  pl.BlockSpec((seq, vw), lambda b, i: (b, 0))],
        out_specs=pl.BlockSpec((tq, vw), lambda b, i: (b * nq + i, 0)),
        out_shape=jax.ShapeDtypeStruct((n, vw), BF16),
        compiler_params=_params(2),
        name="mla_attn",
    )(q, k, v)


def _sb_attn_kernel(qkv_q_ref, k_ref, v_ref, tri_ref, o_ref, *, tq):
    i = pl.program_id(1)
    row = lax.broadcasted_iota(jnp.int32, (tq, tq), 0)
    col = lax.broadcasted_iota(jnp.int32, (tq, tq), 1)
    strict = row > col
    dn = (((1,), (1,)), ((), ()))
    scale = SB_HEAD_DIM ** -0.5
    tri = tri_ref[...]
    outs = []
    for h in range(SB_HEADS):
        hs = slice(h * SB_HEAD_DIM, (h + 1) * SB_HEAD_DIM)
        q = (qkv_q_ref[:, hs].astype(F32) * scale).astype(BF16)

        def block(j, carry, masked):
            run, acc = carry
            start = pl.multiple_of(j * tq, tq)
            z = lax.dot_general(q, k_ref[pl.ds(start, tq), hs], dn, preferred_element_type=F32)
            sp = jnp.maximum(z, 0.0) + jnp.log1p(jnp.exp(-jnp.abs(z)))
            if masked:
                sp = jnp.where(strict, sp, 0.0)
            hi = sp.astype(BF16)
            lo = (sp - hi.astype(F32)).astype(BF16)
            suffix = jnp.dot(jnp.concatenate([hi, lo], axis=1), tri, preferred_element_type=F32)
            a = jnp.exp(z - sp - suffix - run)
            if masked:
                a = jnp.where(strict, a, 0.0)
            acc = acc + jnp.dot(a.astype(BF16), v_ref[pl.ds(start, tq), hs], preferred_element_type=F32)
            run = run + jnp.sum(sp, axis=-1, keepdims=True)
            return run, acc

        carry = (jnp.zeros((tq, 1), F32), jnp.zeros((tq, SB_HEAD_DIM), F32))
        carry = block(i, carry, True)
        carry = lax.fori_loop(0, i, lambda t, c: block(i - 1 - t, c, False), carry)
        outs.append(carry[1])
    o_ref[...] = jnp.concatenate(outs, axis=1).astype(o_ref.dtype)


def _sb_attn(qkv, batch, seq, tq):
    n = qkv.shape[0]
    nq = seq // tq
    j = jnp.arange(tq)
    tri = (j[:, None] > j[None, :]).astype(BF16)
    tri = jnp.concatenate([tri, tri], axis=0)
    return pl.pallas_call(
        functools.partial(_sb_attn_kernel, tq=tq),
        grid=(batch, nq),
        in_specs=[pl.BlockSpec((tq, SB_WIDTH), lambda b, i: (b * nq + i, 0)),
                  pl.BlockSpec((seq, SB_WIDTH), lambda b, i: (b, 1)),
                  pl.BlockSpec((seq, SB_WIDTH), lambda b, i: (b, 2)),
                  _const_spec((2 * tq, tq))],
        out_specs=pl.BlockSpec((tq, SB_WIDTH), lambda b, i: (b * nq + i, 0)),
        out_shape=jax.ShapeDtypeStruct((n, SB_WIDTH), BF16),
        compiler_params=_params(2),
        name="sb_attn",
    )(qkv, qkv, qkv, tri)


def _merge_kernel(x_ref, gate_ref, conv_ref, halo_ref, omla_ref, osb_ref, cw_ref, wa_ref, wb_ref, wc_ref,
                  wo_ref, fn_ref, wr_hi_ref, wr_lo_ref, br_ref, xo_ref, h2_ref, comb_ref, *, tm, tiles_per_seq):
    i = pl.program_id(0)
    f = lambda r: r.astype(F32)
    conv = conv_ref[...]
    u = f(conv[:, 2 * CONV_WIDTH:]) * f(conv[:, :CONV_WIDTH])
    halo = halo_ref[...]
    first = (i % tiles_per_seq) == 0
    up = f(halo[:, 2 * CONV_WIDTH:]) * f(halo[:, :CONV_WIDTH])
    up = jnp.where(first, 0.0, up)
    ue = jnp.concatenate([up, u], axis=0)
    cw = cw_ref[...]
    y = cw[0:1, :] * ue[6:tm + 6] + cw[1:2, :] * ue[7:tm + 7] + cw[2:3, :] * u
    ya = (f(conv[:, CONV_WIDTH:2 * CONV_WIDTH]) * y).astype(BF16)
    gate = gate_ref[...]
    merged = (f(gate[:, :D_MODEL]) * jnp.dot(ya, wa_ref[...], preferred_element_type=F32)
              + f(gate[:, D_MODEL:2 * D_MODEL]) * jnp.dot(omla_ref[...], wb_ref[...], preferred_element_type=F32)
              + f(gate[:, 2 * D_MODEL:]) * jnp.dot(osb_ref[...], wc_ref[...], preferred_element_type=F32))
    x_new = x_ref[...] + jnp.dot(merged.astype(BF16), wo_ref[...], preferred_element_type=F32)
    xo_ref[...] = x_new
    h2 = _rms(x_new, fn_ref[...])
    h2_hi = h2.astype(BF16)
    h2_ref[...] = h2_hi
    h2_lo = (h2 - h2_hi.astype(F32)).astype(BF16)
    logits = (jnp.dot(h2_hi, wr_hi_ref[...], preferred_element_type=F32)
              + jnp.dot(h2_lo, wr_hi_ref[...], preferred_element_type=F32)
              + jnp.dot(h2_hi, wr_lo_ref[...], preferred_element_type=F32)) + br_ref[...]
    comb_ref[...] = _route(logits)


def _route(logits):
    lane = lax.broadcasted_iota(jnp.int32, logits.shape, 1)
    big = jnp.int32(1 << 30)
    neg = -jnp.inf
    is_group = (lane >= N_EXPERTS) & (lane < N_EXPERTS + N_GROUPS)
    gl = jnp.where(is_group, logits, neg)
    gmax = jnp.max(gl, axis=-1, keepdims=True)
    g_w = 1.0 / jnp.sum(jnp.exp(gl - gmax), axis=-1, keepdims=True)
    g_idx = jnp.min(jnp.where(gl == gmax, lane, big), axis=-1, keepdims=True) - N_EXPERTS
    lo = g_idx * EXPERTS_PER_GROUP
    el = jnp.where((lane >= lo) & (lane < lo + EXPERTS_PER_GROUP), logits, neg)
    m1 = jnp.max(el, axis=-1, keepdims=True)
    i1 = jnp.min(jnp.where(el == m1, lane, big), axis=-1, keepdims=True)
    el2 = jnp.where(lane == i1, neg, el)
    m2 = jnp.max(el2, axis=-1, keepdims=True)
    i2 = jnp.min(jnp.where(el2 == m2, lane, big), axis=-1, keepdims=True)
    r = jnp.exp(m2 - m1)
    w1 = g_w / (1.0 + r)
    w2 = g_w * r / (1.0 + r)
    return jnp.where(lane == i1, w1, 0.0) + jnp.where(lane == i2, w2, 0.0)


def _merge(x2, gates, conv, o_mla, o_sb, conv_w, wa, wb, wc, wo, ffn_norm, wr_hi, wr_lo, b_router, seq, tm):
    n = x2.shape[0]
    row = lambda w: pl.BlockSpec((tm, w), lambda i: (i, 0))
    sub = tm // 8
    halo = pl.BlockSpec((8, CONV_COLS), lambda i: (jnp.maximum(i * sub - 1, 0), 0))
    return pl.pallas_call(
        functools.partial(_merge_kernel, tm=tm, tiles_per_seq=seq // tm),
        grid=(n // tm,),
        in_specs=[row(D_MODEL), row(GATE_COLS), row(CONV_COLS), halo, row(CONV_WIDTH), row(SB_WIDTH),
                  _const_spec((8, CONV_WIDTH)), _const_spec((CONV_WIDTH, D_MODEL)),
                  _const_spec((MLA_HEADS * MLA_V, D_MODEL)), _const_spec((SB_WIDTH, D_MODEL)),
                  _const_spec((D_MODEL, D_MODEL)), _const_spec((1, D_MODEL)),
                  _const_spec((D_MODEL, ROUTER_COLS)), _const_spec((D_MODEL, ROUTER_COLS)),
                  _const_spec((1, ROUTER_COLS))],
        out_specs=[row(D_MODEL), row(D_MODEL), row(ROUTER_COLS)],
        out_shape=[jax.ShapeDtypeStruct((n, D_MODEL), F32), jax.ShapeDtypeStruct((n, D_MODEL), BF16),
                   jax.ShapeDtypeStruct((n, ROUTER_COLS), F32)],
        compiler_params=_params(1),
        name="merge",
    )(x2, gates, conv, conv, o_mla, o_sb, conv_w, wa, wb, wc, wo, ffn_norm, wr_hi, wr_lo, b_router)


def _pack_router(w_rg, b_rg, w_re, b_re):
    pad = ROUTER_COLS - N_EXPERTS - N_GROUPS
    w = jnp.concatenate([w_re, w_rg, jnp.zeros((D_MODEL, pad), F32)], axis=1)
    b = jnp.concatenate([b_re, b_rg, jnp.zeros((pad,), F32)])[None, :]
    hi = w.astype(BF16)
    lo = (w - hi.astype(F32)).astype(BF16)
    return hi, lo, b


def _moe_kernel(x_ref, h_ref, comb_ref, wg_ref, wu_ref, wd_ref, fin_ref, o_ref, *, final):
    e = pl.program_id(1)

    @pl.when(e == 0)
    def _():
        o_ref[...] = x_ref[...]

    h = h_ref[...]
    g = jnp.dot(h, wg_ref[0], preferred_element_type=F32)
    u = jnp.dot(h, wu_ref[0], preferred_element_type=F32)
    act = (g * jax.nn.sigmoid(g) * u).astype(BF16)
    lane = lax.broadcasted_iota(jnp.int32, comb_ref.shape, 1)
    c = jnp.sum(jnp.where(lane == e, comb_ref[...], 0.0), axis=-1, keepdims=True)
    o_ref[...] += c * jnp.dot(act, wd_ref[0], preferred_element_type=F32)

    if final:
        @pl.when(e == N_EXPERTS - 1)
        def _():
            o_ref[...] = _rms(o_ref[...], fin_ref[...])


def _moe(x2, h2, comb, wg, wu, wd, final_norm, final, tm):
    n = x2.shape[0]
    row = lambda w: pl.BlockSpec((tm, w), lambda i, e: (i, 0))
    return pl.pallas_call(
        functools.partial(_moe_kernel, final=final),
        grid=(n // tm, N_EXPERTS),
        in_specs=[row(D_MODEL), row(D_MODEL), row(ROUTER_COLS),
                  pl.BlockSpec((1, D_MODEL, D_EXPERT), lambda i, e: (e, 0, 0)),
                  pl.BlockSpec((1, D_MODEL, D_EXPERT), lambda i, e: (e, 0, 0)),
                  pl.BlockSpec((1, D_EXPERT, D_MODEL), lambda i, e: (e, 0, 0)),
                  pl.BlockSpec((1, D_MODEL), lambda i, e: (0, 0))],
        out_specs=row(D_MODEL),
        out_shape=jax.ShapeDtypeStruct((n, D_MODEL), F32),
        compiler_params=_params(2),
        name="moe",
    )(x2, h2, comb, wg, wu, wd, final_norm)


def _tile(n, pref):
    t = min(n, pref)
    assert n % t == 0, (n, t)
    return t


def kernel(x, positions, attn_norm, w_in, b_gate, conv_w, w_out_conv, q_norm, kv_norm, w_uq, w_ukv, w_out_mla, w_out_sb, w_o, ffn_norm, w_router_group, b_router_group, w_router_expert, b_router_expert, w_exp_gate, w_exp_up, w_exp_down, final_norm):
    batch, seq, d = x.shape
    assert d == D_MODEL
    n = batch * seq
    depth = w_in.shape[0]
    tm_rows = _tile(seq, 512)
    tq = _tile(seq, 256)
    x2 = x.reshape(n, d)
    cos, sin = _rope_tables(positions, _tile(n, 1024))
    for l in range(depth):
        gates, conv, lat, sb = _in_proj(x2, attn_norm[l][None, :], _pack_w_in(w_in[l]), b_gate[l][None, :], tm_rows)
        wq_main, wq_swap, wk, wv = _pack_mla_weights(w_uq[l], w_ukv[l])
        q, k, v = _mla_prep(lat, cos, sin, q_norm[l][None, :], kv_norm[l][None, :], wq_main, wq_swap, wk, wv,
                            tm_rows)
        o_mla = _mla_attn(q, k, v, batch, seq, tq)
        o_sb = _sb_attn(sb, batch, seq, tq)
        wr_hi, wr_lo, b_router = _pack_router(w_router_group[l], b_router_group[l], w_router_expert[l],
                                              b_router_expert[l])
        conv_w8 = jnp.concatenate([conv_w[l], jnp.zeros((8 - CONV_K, CONV_WIDTH), F32)], axis=0)
        x_mid, h2, comb = _merge(x2, gates, conv, o_mla, o_sb, conv_w8, w_out_conv[l].astype(BF16),
                                 w_out_mla[l].astype(BF16), w_out_sb[l].astype(BF16), w_o[l].astype(BF16),
                                 ffn_norm[l][None, :], wr_hi, wr_lo, b_router, seq, tm_rows)
        x2 = _moe(x_mid, h2, comb, w_exp_gate[l].astype(BF16), w_exp_up[l].astype(BF16),
                  w_exp_down[l].astype(BF16), final_norm[None, :], l == depth - 1, _tile(n, 1024))
    return x2.reshape(batch, seq, d)
```

```python
import functools

import jax
import jax.numpy as jnp
from jax import lax
from jax.experimental import pallas as pl
from jax.experimental.pallas import tpu as pltpu

D_MODEL = 1024
CONV_WIDTH = 512
CONV_K = 3
MLA_HEADS = 8
MLA_NOPE = 64
MLA_ROPE = 32
MLA_V = 64
MLA_Q_RANK = 256
MLA_KV_RANK = 128
ROPE_THETA = 10000.0
SB_HEADS = 8
SB_HEAD_DIM = 64
SB_WIDTH = SB_HEADS * SB_HEAD_DIM
N_BRANCHES = 3
OFF_CONV = 0
OFF_CQ = OFF_CONV + 3 * CONV_WIDTH
OFF_CKV = OFF_CQ + MLA_Q_RANK
OFF_KR = OFF_CKV + MLA_KV_RANK
OFF_SB = OFF_KR + MLA_ROPE
OFF_GATE = OFF_SB + 3 * SB_WIDTH
N_GROUPS = 4
EXPERTS_PER_GROUP = 8
N_EXPERTS = N_GROUPS * EXPERTS_PER_GROUP
D_EXPERT = 256
EPS = 1e-6
LOG2E = 1.4426950408889634

LANES = 128
HEAD_PAD = 128
ROPE_HALF = MLA_ROPE // 2
GATE_COLS = N_BRANCHES * D_MODEL
CONV_COLS = 3 * CONV_WIDTH
LAT_COLS = MLA_Q_RANK + MLA_KV_RANK + 2 * HEAD_PAD
SB_COLS = 3 * SB_WIDTH
ROUTER_COLS = LANES
VMEM_LIMIT = 56 * 1024 * 1024

BF16 = jnp.bfloat16
F32 = jnp.float32


def _params(n_axes, vmem=VMEM_LIMIT):
    return pltpu.CompilerParams(dimension_semantics=("arbitrary",) * n_axes, vmem_limit_bytes=vmem)


def _rms(xf, gain):
    return xf * lax.rsqrt(jnp.mean(xf * xf, axis=-1, keepdims=True) + EPS) * gain


def _const_spec(shape):
    return pl.BlockSpec(shape, lambda *_: (0,) * len(shape))


def _rope_table_kernel(pos_ref, freq_ref, cos_ref, sin_ref):
    ang = pos_ref[...].astype(F32) * freq_ref[...]
    cos_ref[...] = jnp.cos(ang)
    sin_ref[...] = jnp.sin(ang)


def _rope_tables(positions, tm):
    n = positions.size
    half = ROPE_HALF
    freqs = ROPE_THETA ** (-jnp.arange(half, dtype=F32) / half)
    zeros = jnp.zeros((MLA_NOPE,), F32)
    freq_row = jnp.concatenate([zeros, freqs, freqs, jnp.zeros((HEAD_PAD - MLA_NOPE - MLA_ROPE,), F32)])[None, :]
    pos = positions.reshape(n, 1)
    return pl.pallas_call(
        _rope_table_kernel,
        grid=(n // tm,),
        in_specs=[pl.BlockSpec((tm, 1), lambda i: (i, 0)), _const_spec((1, HEAD_PAD))],
        out_specs=[pl.BlockSpec((tm, HEAD_PAD), lambda i: (i, 0))] * 2,
        out_shape=[jax.ShapeDtypeStruct((n, HEAD_PAD), F32)] * 2,
        compiler_params=_params(1),
        name="rope_tables",
    )(pos, freq_row)


def _in_proj_kernel(x_ref, g_ref, w_ref, b_ref, gate_ref, conv_ref, lat_ref, sb_ref, *, chunk):
    h = _rms(x_ref[...], g_ref[...]).astype(BF16)

    def run(out_ref, col0, width, epilogue):
        for c in range(0, width, chunk):
            cw = min(chunk, width - c)
            acc = jnp.dot(h, w_ref[:, col0 + c:col0 + c + cw], preferred_element_type=F32)
            out_ref[:, c:c + cw] = epilogue(acc, c, cw).astype(out_ref.dtype)

    run(gate_ref, 0, GATE_COLS, lambda a, c, cw: jax.nn.sigmoid(a + b_ref[:, c:c + cw]))
    ident = lambda a, c, cw: a
    run(conv_ref, GATE_COLS, CONV_COLS, ident)
    run(lat_ref, GATE_COLS + CONV_COLS, LAT_COLS, ident)
    run(sb_ref, GATE_COLS + CONV_COLS + LAT_COLS, SB_COLS, ident)


def _in_proj(x2, gain, w_packed, b_gate, tm):
    n = x2.shape[0]
    total = w_packed.shape[1]
    widths = (GATE_COLS, CONV_COLS, LAT_COLS, SB_COLS)
    return pl.pallas_call(
        functools.partial(_in_proj_kernel, chunk=512),
        grid=(n // tm,),
        in_specs=[pl.BlockSpec((tm, D_MODEL), lambda i: (i, 0)), _const_spec((1, D_MODEL)),
                  _const_spec((D_MODEL, total)), _const_spec((1, GATE_COLS))],
        out_specs=[pl.BlockSpec((tm, w), lambda i: (i, 0)) for w in widths],
        out_shape=[jax.ShapeDtypeStruct((n, w), BF16) for w in widths],
        compiler_params=_params(1),
        name="in_proj",
    )(x2, gain, w_packed, b_gate)


def _pack_w_in(w):
    kr = w[:, OFF_KR:OFF_SB]
    x1, x2 = kr[:, :ROPE_HALF], kr[:, ROPE_HALF:]
    z_lo = jnp.zeros((D_MODEL, MLA_NOPE), w.dtype)
    z_hi = jnp.zeros((D_MODEL, HEAD_PAD - MLA_NOPE - MLA_ROPE), w.dtype)
    kr_main = jnp.concatenate([z_lo, x1, x2, z_hi], axis=1)
    kr_swap = jnp.concatenate([z_lo, -x2, x1, z_hi], axis=1)
    q_sb = w[:, OFF_SB:OFF_SB + SB_WIDTH] * (LOG2E * SB_HEAD_DIM ** -0.5)
    return jnp.concatenate([w[:, OFF_GATE:], w[:, OFF_CONV:OFF_CQ], w[:, OFF_CQ:OFF_KR], kr_main, kr_swap,
                            q_sb, w[:, OFF_SB + SB_WIDTH:OFF_GATE]], axis=1).astype(BF16)


def _mla_prep_kernel(lat_ref, cos_ref, sin_ref, qn_ref, kvn_ref, wqm_ref, wqs_ref, wk_ref, wv_ref,
                     q_ref, k_ref, v_ref, *, scale):
    cos, sin = cos_ref[...], sin_ref[...]
    cq = _rms(lat_ref[:, :MLA_Q_RANK].astype(F32), qn_ref[...]).astype(BF16)
    ckv = _rms(lat_ref[:, MLA_Q_RANK:MLA_Q_RANK + MLA_KV_RANK].astype(F32), kvn_ref[...]).astype(BF16)
    kr0 = MLA_Q_RANK + MLA_KV_RANK
    kr = (lat_ref[:, kr0:kr0 + HEAD_PAD].astype(F32) * cos
          + lat_ref[:, kr0 + HEAD_PAD:kr0 + 2 * HEAD_PAD].astype(F32) * sin)
    qm = jnp.dot(cq, wqm_ref[...], preferred_element_type=F32)
    qs = jnp.dot(cq, wqs_ref[...], preferred_element_type=F32)
    kn = jnp.dot(ckv, wk_ref[...], preferred_element_type=F32)
    for h in range(MLA_HEADS):
        sl = slice(h * HEAD_PAD, (h + 1) * HEAD_PAD)
        q_ref[:, sl] = ((qm[:, sl] * cos + qs[:, sl] * sin) * scale).astype(BF16)
        k_ref[:, sl] = (kn[:, sl] + kr).astype(BF16)
    v = jnp.dot(ckv, wv_ref[...], preferred_element_type=F32)
    lane = lax.broadcasted_iota(jnp.int32, v.shape, 1)
    v_ref[...] = jnp.where(lane % HEAD_PAD == MLA_V, 1.0, v).astype(BF16)


def _mla_prep(lat, cos, sin, q_norm, kv_norm, wq_main, wq_swap, wk, wv, tm):
    n = lat.shape[0]
    hw = MLA_HEADS * HEAD_PAD
    row = lambda w: pl.BlockSpec((tm, w), lambda i: (i, 0))
    return pl.pallas_call(
        functools.partial(_mla_prep_kernel, scale=LOG2E * (MLA_NOPE + MLA_ROPE) ** -0.5),
        grid=(n // tm,),
        in_specs=[row(LAT_COLS), row(HEAD_PAD), row(HEAD_PAD), _const_spec((1, MLA_Q_RANK)),
                  _const_spec((1, MLA_KV_RANK)), _const_spec((MLA_Q_RANK, hw)), _const_spec((MLA_Q_RANK, hw)),
                  _const_spec((MLA_KV_RANK, hw)), _const_spec((MLA_KV_RANK, hw))],
        out_specs=[row(hw), row(hw), row(hw)],
        out_shape=[jax.ShapeDtypeStruct((n, hw), BF16)] * 3,
        compiler_params=_params(1),
        name="mla_prep",
    )(lat, cos, sin, q_norm, kv_norm, wq_main, wq_swap, wk, wv)


def _pack_mla_weights(w_uq, w_ukv):
    qd = MLA_NOPE + MLA_ROPE
    z_hi = jnp.zeros((MLA_Q_RANK, HEAD_PAD - qd), w_uq.dtype)
    z_lo = jnp.zeros((MLA_Q_RANK, MLA_NOPE), w_uq.dtype)
    main, swap, wk, wv = [], [], [], []
    for h in range(MLA_HEADS):
        wq = w_uq[:, h * qd:(h + 1) * qd]
        nope, x1, x2 = wq[:, :MLA_NOPE], wq[:, MLA_NOPE:MLA_NOPE + ROPE_HALF], wq[:, MLA_NOPE + ROPE_HALF:]
        main += [nope, x1, x2, z_hi]
        swap += [z_lo, -x2, x1, z_hi]
        kv = w_ukv[:, h * (MLA_NOPE + MLA_V):(h + 1) * (MLA_NOPE + MLA_V)]
        wk += [kv[:, :MLA_NOPE], jnp.zeros((MLA_KV_RANK, HEAD_PAD - MLA_NOPE), w_ukv.dtype)]
        wv += [kv[:, MLA_NOPE:], jnp.zeros((MLA_KV_RANK, HEAD_PAD - MLA_V), w_ukv.dtype)]
    cat = lambda parts: jnp.concatenate(parts, axis=1).astype(BF16)
    return cat(main), cat(swap), cat(wk), cat(wv)


def _mla_attn_kernel(q_ref, k_ref, v_ref, o_ref, m_ref, acc_ref, *, tq):
    i = pl.program_id(1)
    row = lax.broadcasted_iota(jnp.int32, (tq, tq), 0)
    col = lax.broadcasted_iota(jnp.int32, (tq, tq), 1)
    causal = row >= col
    dn = (((1,), (1,)), ((), ()))

    def step(j, diagonal):
        start = pl.multiple_of(j * tq, tq)
        for h in range(MLA_HEADS):
            hs = slice(h * HEAD_PAD, (h + 1) * HEAD_PAD)
            s = lax.dot_general(q_ref[:, hs], k_ref[pl.ds(start, tq), hs], dn, preferred_element_type=F32)
            if diagonal:
                s = jnp.where(causal, s, -jnp.inf)
            m_cur = jnp.max(s, axis=1, keepdims=True)
            if diagonal:
                m_new = jnp.broadcast_to(m_cur, (tq, LANES))
            else:
                m_old = m_ref[h]
                m_new = jnp.maximum(m_old, m_cur)
            p = jnp.exp2(s - pltpu.repeat(m_new, tq // LANES, axis=1))
            pv = jnp.dot(p.astype(BF16), v_ref[pl.ds(start, tq), hs], preferred_element_type=F32)
            if diagonal:
                acc_ref[h] = pv
            else:
                acc_ref[h] = jnp.exp2(m_old - m_new) * acc_ref[h] + pv
            m_ref[h] = m_new

    step(i, True)

    def body(j, carry):
        step(j, False)
        return carry

    lax.fori_loop(0, i, body, 0)
    outs = []
    for h in range(MLA_HEADS):
        acc = acc_ref[h]
        outs.append(acc[:, :MLA_V] / acc[:, MLA_V:MLA_V + 1])
    o_ref[...] = jnp.concatenate(outs, axis=1).astype(o_ref.dtype)


def _mla_attn(q, k, v, batch, seq, tq):
    n = q.shape[0]
    hw = MLA_HEADS * HEAD_PAD
    vw = MLA_HEADS * MLA_V
    nq = seq // tq
    return pl.pallas_call(
        functools.partial(_mla_attn_kernel, tq=tq),
        grid=(batch, nq),
        in_specs=[pl.BlockSpec((tq, hw), lambda b, i: (b * nq + i, 0)),
                  pl.BlockSpec((seq, hw), lambda b, i: (b, 0)),
                  pl.BlockSpec((seq, hw), lambda b, i: (b, 0))],
        out_specs=pl.BlockSpec((tq, vw), lambda b, i: (b * nq + i, 0)),
        out_shape=jax.ShapeDtypeStruct((n, vw), BF16),
        scratch_shapes=[pltpu.VMEM((MLA_HEADS, tq, LANES), F32), pltpu.VMEM((MLA_HEADS, tq, HEAD_PAD), F32)],
        compiler_params=_params(2),
        name="mla_attn",
    )(q, k, v)


def _sb_attn_kernel(q_ref, k_ref, v_ref, tri_ref, o_ref, run_ref, acc_ref, *, tq):
    i = pl.program_id(1)
    row = lax.broadcasted_iota(jnp.int32, (tq, tq), 0)
    col = lax.broadcasted_iota(jnp.int32, (tq, tq), 1)
    strict = row > col
    dn = (((1,), (1,)), ((), ()))
    tri = tri_ref[...]
    sign = jnp.uint32(0x80000000)

    def step(j, diagonal):
        start = pl.multiple_of(j * tq, tq)
        for h in range(SB_HEADS):
            hs = slice(h * SB_HEAD_DIM, (h + 1) * SB_HEAD_DIM)
            z = lax.dot_general(q_ref[:, hs], k_ref[pl.ds(start, tq), hs], dn, preferred_element_type=F32)
            neg_abs = pltpu.bitcast(pltpu.bitcast(z, jnp.uint32) | sign, F32)
            sp = jnp.maximum(z, 0.0) + jnp.log(1.0 + jnp.exp2(neg_abs)) * LOG2E
            if diagonal:
                sp = jnp.where(strict, sp, 0.0)
            later = jnp.dot(sp.astype(BF16), tri, preferred_element_type=F32)
            a = jnp.exp2(z - sp - later)
            if diagonal:
                a = jnp.where(strict, a, 0.0)
            av = jnp.dot(a.astype(BF16), v_ref[pl.ds(start, tq), hs], preferred_element_type=F32)
            total = jnp.sum(sp, axis=1, keepdims=True)
            if diagonal:
                acc_ref[h] = av
                run_ref[h] = jnp.broadcast_to(total, (tq, LANES))
            else:
                run = run_ref[h]
                acc_ref[h] += jnp.exp2(-run[:, :SB_HEAD_DIM]) * av
                run_ref[h] = run + total

    step(i, True)

    def body(t, carry):
        step(i - 1 - t, False)
        return carry

    lax.fori_loop(0, i, body, 0)
    o_ref[...] = jnp.concatenate([acc_ref[h] for h in range(SB_HEADS)], axis=1).astype(o_ref.dtype)


def _sb_attn(qkv, batch, seq, tq):
    n = qkv.shape[0]
    nq = seq // tq
    j = jnp.arange(tq)
    tri = (j[:, None] > j[None, :]).astype(BF16)
    return pl.pallas_call(
        functools.partial(_sb_attn_kernel, tq=tq),
        grid=(batch, nq),
        in_specs=[pl.BlockSpec((tq, SB_WIDTH), lambda b, i: (b * nq + i, 0)),
                  pl.BlockSpec((seq, SB_WIDTH), lambda b, i: (b, 1)),
                  pl.BlockSpec((seq, SB_WIDTH), lambda b, i: (b, 2)),
                  _const_spec((tq, tq))],
        out_specs=pl.BlockSpec((tq, SB_WIDTH), lambda b, i: (b * nq + i, 0)),
        out_shape=jax.ShapeDtypeStruct((n, SB_WIDTH), BF16),
        scratch_shapes=[pltpu.VMEM((SB_HEADS, tq, LANES), F32), pltpu.VMEM((SB_HEADS, tq, SB_HEAD_DIM), F32)],
        compiler_params=_params(2),
        name="sb_attn",
    )(qkv, qkv, qkv, tri)


def _merge_kernel(x_ref, gate_ref, conv_ref, halo_ref, omla_ref, osb_ref, cw_ref, wa_ref, wb_ref, wc_ref,
                  wo_ref, fn_ref, wr_hi_ref, wr_lo_ref, br_ref, xo_ref, h2_ref, comb_ref, *, tm, tiles_per_seq):
    i = pl.program_id(0)
    f = lambda r: r.astype(F32)
    conv = conv_ref[...]
    u = f(conv[:, 2 * CONV_WIDTH:]) * f(conv[:, :CONV_WIDTH])
    halo = halo_ref[...]
    first = (i % tiles_per_seq) == 0
    up = f(halo[:, 2 * CONV_WIDTH:]) * f(halo[:, :CONV_WIDTH])
    up = jnp.where(first, 0.0, up)
    ue = jnp.concatenate([up, u], axis=0)
    cw = cw_ref[...]
    y = cw[0:1, :] * ue[6:tm + 6] + cw[1:2, :] * ue[7:tm + 7] + cw[2:3, :] * u
    ya = (f(conv[:, CONV_WIDTH:2 * CONV_WIDTH]) * y).astype(BF16)
    gate = gate_ref[...]
    merged = (f(gate[:, :D_MODEL]) * jnp.dot(ya, wa_ref[...], preferred_element_type=F32)
              + f(gate[:, D_MODEL:2 * D_MODEL]) * jnp.dot(omla_ref[...], wb_ref[...], preferred_element_type=F32)
              + f(gate[:, 2 * D_MODEL:]) * jnp.dot(osb_ref[...], wc_ref[...], preferred_element_type=F32))
    x_new = x_ref[...] + jnp.dot(merged.astype(BF16), wo_ref[...], preferred_element_type=F32)
    xo_ref[...] = x_new
    h2 = _rms(x_new, fn_ref[...])
    h2_hi = h2.astype(BF16)
    h2_ref[...] = h2_hi
    h2_lo = (h2 - h2_hi.astype(F32)).astype(BF16)
    logits = (jnp.dot(h2_hi, wr_hi_ref[...], preferred_element_type=F32)
              + jnp.dot(h2_lo, wr_hi_ref[...], preferred_element_type=F32)
              + jnp.dot(h2_hi, wr_lo_ref[...], preferred_element_type=F32)) + br_ref[...]
    comb_ref[...] = _route(logits)


def _route(logits):
    lane = lax.broadcasted_iota(jnp.int32, logits.shape, 1)
    big = jnp.int32(1 << 30)
    neg = -jnp.inf
    is_group = (lane >= N_EXPERTS) & (lane < N_EXPERTS + N_GROUPS)
    gl = jnp.where(is_group, logits, neg)
    gmax = jnp.max(gl, axis=-1, keepdims=True)
    g_w = 1.0 / jnp.sum(jnp.exp(gl - gmax), axis=-1, keepdims=True)
    g_idx = jnp.min(jnp.where(gl == gmax, lane, big), axis=-1, keepdims=True) - N_EXPERTS
    lo = g_idx * EXPERTS_PER_GROUP
    el = jnp.where((lane >= lo) & (lane < lo + EXPERTS_PER_GROUP), logits, neg)
    m1 = jnp.max(el, axis=-1, keepdims=True)
    i1 = jnp.min(jnp.where(el == m1, lane, big), axis=-1, keepdims=True)
    el2 = jnp.where(lane == i1, neg, el)
    m2 = jnp.max(el2, axis=-1, keepdims=True)
    i2 = jnp.min(jnp.where(el2 == m2, lane, big), axis=-1, keepdims=True)
    r = jnp.exp(m2 - m1)
    w1 = g_w / (1.0 + r)
    w2 = g_w * r / (1.0 + r)
    return jnp.where(lane == i1, w1, 0.0) + jnp.where(lane == i2, w2, 0.0)


def _merge(x2, gates, conv, o_mla, o_sb, conv_w, wa, wb, wc, wo, ffn_norm, wr_hi, wr_lo, b_router, seq, tm):
    n = x2.shape[0]
    row = lambda w: pl.BlockSpec((tm, w), lambda i: (i, 0))
    sub = tm // 8
    halo = pl.BlockSpec((8, CONV_COLS), lambda i: (jnp.maximum(i * sub - 1, 0), 0))
    return pl.pallas_call(
        functools.partial(_merge_kernel, tm=tm, tiles_per_seq=seq // tm),
        grid=(n // tm,),
        in_specs=[row(D_MODEL), row(GATE_COLS), row(CONV_COLS), halo, row(CONV_WIDTH), row(SB_WIDTH),
                  _const_spec((8, CONV_WIDTH)), _const_spec((CONV_WIDTH, D_MODEL)),
                  _const_spec((MLA_HEADS * MLA_V, D_MODEL)), _const_spec((SB_WIDTH, D_MODEL)),
                  _const_spec((D_MODEL, D_MODEL)), _const_spec((1, D_MODEL)),
                  _const_spec((D_MODEL, ROUTER_COLS)), _const_spec((D_MODEL, ROUTER_COLS)),
                  _const_spec((1, ROUTER_COLS))],
        out_specs=[row(D_MODEL), row(D_MODEL), row(ROUTER_COLS)],
        out_shape=[jax.ShapeDtypeStruct((n, D_MODEL), F32), jax.ShapeDtypeStruct((n, D_MODEL), BF16),
                   jax.ShapeDtypeStruct((n, ROUTER_COLS), F32)],
        compiler_params=_params(1),
        name="merge",
    )(x2, gates, conv, conv, o_mla, o_sb, conv_w, wa, wb, wc, wo, ffn_norm, wr_hi, wr_lo, b_router)


def _pack_router(w_rg, b_rg, w_re, b_re):
    pad = ROUTER_COLS - N_EXPERTS - N_GROUPS
    w = jnp.concatenate([w_re, w_rg, jnp.zeros((D_MODEL, pad), F32)], axis=1)
    b = jnp.concatenate([b_re, b_rg, jnp.zeros((pad,), F32)])[None, :]
    hi = w.astype(BF16)
    lo = (w - hi.astype(F32)).astype(BF16)
    return hi, lo, b


def _moe_kernel(x_ref, h_ref, comb_ref, wg_ref, wu_ref, wd_ref, fin_ref, o_ref, *, final):
    e = pl.program_id(1)

    @pl.when(e == 0)
    def _():
        o_ref[...] = x_ref[...]

    h = h_ref[...]
    g = jnp.dot(h, wg_ref[0], preferred_element_type=F32)
    u = jnp.dot(h, wu_ref[0], preferred_element_type=F32)
    act = (g * jax.nn.sigmoid(g) * u).astype(BF16)
    lane = lax.broadcasted_iota(jnp.int32, comb_ref.shape, 1)
    c = jnp.sum(jnp.where(lane == e, comb_ref[...], 0.0), axis=-1, keepdims=True)
    o_ref[...] += c * jnp.dot(act, wd_ref[0], preferred_element_type=F32)

    if final:
        @pl.when(e == N_EXPERTS - 1)
        def _():
            o_ref[...] = _rms(o_ref[...], fin_ref[...])


def _moe(x2, h2, comb, wg, wu, wd, final_norm, final, tm):
    n = x2.shape[0]
    row = lambda w: pl.BlockSpec((tm, w), lambda i, e: (i, 0))
    return pl.pallas_call(
        functools.partial(_moe_kernel, final=final),
        grid=(n // tm, N_EXPERTS),
        in_specs=[row(D_MODEL), row(D_MODEL), row(ROUTER_COLS),
                  pl.BlockSpec((1, D_MODEL, D_EXPERT), lambda i, e: (e, 0, 0)),
                  pl.BlockSpec((1, D_MODEL, D_EXPERT), lambda i, e: (e, 0, 0)),
                  pl.BlockSpec((1, D_EXPERT, D_MODEL), lambda i, e: (e, 0, 0)),
                  pl.BlockSpec((1, D_MODEL), lambda i, e: (0, 0))],
        out_specs=row(D_MODEL),
        out_shape=jax.ShapeDtypeStruct((n, D_MODEL), F32),
        compiler_params=_params(2),
        name="moe",
    )(x2, h2, comb, wg, wu, wd, final_norm)


def _tile(n, pref):
    t = min(n, pref)
    assert n % t == 0, (n, t)
    return t


def kernel(x, positions, attn_norm, w_in, b_gate, conv_w, w_out_conv, q_norm, kv_norm, w_uq, w_ukv, w_out_mla, w_out_sb, w_o, ffn_norm, w_router_group, b_router_group, w_router_expert, b_router_expert, w_exp_gate, w_exp_up, w_exp_down, final_norm):
    batch, seq, d = x.shape
    assert d == D_MODEL
    n = batch * seq
    depth = w_in.shape[0]
    tm_rows = _tile(seq, 512)
    tq = _tile(seq, 256)
    x2 = x.reshape(n, d)
    cos, sin = _rope_tables(positions, _tile(n, 1024))
    for l in range(depth):
        gates, conv, lat, sb = _in_proj(x2, attn_norm[l][None, :], _pack_w_in(w_in[l]), b_gate[l][None, :], tm_rows)
        wq_main, wq_swap, wk, wv = _pack_mla_weights(w_uq[l], w_ukv[l])
        q, k, v = _mla_prep(lat, cos, sin, q_norm[l][None, :], kv_norm[l][None, :], wq_main, wq_swap, wk, wv,
                            tm_rows)
        o_mla = _mla_attn(q, k, v, batch, seq, tq)
        o_sb = _sb_attn(sb, batch, seq, tq)
        wr_hi, wr_lo, b_router = _pack_router(w_router_group[l], b_router_group[l], w_router_expert[l],
                                              b_router_expert[l])
        conv_w8 = jnp.concatenate([conv_w[l], jnp.zeros((8 - CONV_K, CONV_WIDTH), F32)], axis=0)
        x_mid, h2, comb = _merge(x2, gates, conv, o_mla, o_sb, conv_w8, w_out_conv[l].astype(BF16),
                                 w_out_mla[l].astype(BF16), w_out_sb[l].astype(BF16), w_o[l].astype(BF16),
                                 ffn_norm[l][None, :], wr_hi, wr_lo, b_router, seq, tm_rows)
        x2 = _moe(x_mid, h2, comb, w_exp_gate[l].astype(BF16), w_exp_up[l].astype(BF16),
                  w_exp_down[l].astype(BF16), final_norm[None, :], l == depth - 1, _tile(n, 1024))
    return x2.reshape(batch, seq, d)
```

```python
import functools

import jax
import jax.numpy as jnp
from jax import lax
from jax.experimental import pallas as pl
from jax.experimental.pallas import tpu as pltpu

D_MODEL = 1024
CONV_WIDTH = 512
CONV_K = 3
MLA_HEADS = 8
MLA_NOPE = 64
MLA_ROPE = 32
MLA_V = 64
MLA_Q_RANK = 256
MLA_KV_RANK = 128
ROPE_THETA = 10000.0
SB_HEADS = 8
SB_HEAD_DIM = 64
SB_WIDTH = SB_HEADS * SB_HEAD_DIM
N_BRANCHES = 3
OFF_CONV = 0
OFF_CQ = OFF_CONV + 3 * CONV_WIDTH
OFF_CKV = OFF_CQ + MLA_Q_RANK
OFF_KR = OFF_CKV + MLA_KV_RANK
OFF_SB = OFF_KR + MLA_ROPE
OFF_GATE = OFF_SB + 3 * SB_WIDTH
N_GROUPS = 4
EXPERTS_PER_GROUP = 8
N_EXPERTS = N_GROUPS * EXPERTS_PER_GROUP
D_EXPERT = 256
EPS = 1e-6
LOG2E = 1.4426950408889634

LANES = 128
HEAD_PAD = 128
ROPE_HALF = MLA_ROPE // 2
GATE_COLS = N_BRANCHES * D_MODEL
CONV_COLS = 3 * CONV_WIDTH
LAT_COLS = MLA_Q_RANK + MLA_KV_RANK + 2 * HEAD_PAD
SB_COLS = 3 * SB_WIDTH
ROUTER_COLS = LANES
ROW_TILES = D_MODEL // LANES
MOE_TILE = 256
VMEM_LIMIT = 56 * 1024 * 1024

BF16 = jnp.bfloat16
F32 = jnp.float32


def _params(n_axes, vmem=VMEM_LIMIT):
    return pltpu.CompilerParams(dimension_semantics=("arbitrary",) * n_axes, vmem_limit_bytes=vmem)


def _rms(xf, gain):
    return xf * lax.rsqrt(jnp.mean(xf * xf, axis=-1, keepdims=True) + EPS) * gain


def _const_spec(shape):
    return pl.BlockSpec(shape, lambda *_: (0,) * len(shape))


def _rope_table_kernel(pos_ref, freq_ref, cos_ref, sin_ref):
    ang = pos_ref[...].astype(F32) * freq_ref[...]
    cos_ref[...] = jnp.cos(ang)
    sin_ref[...] = jnp.sin(ang)


def _rope_tables(positions, tm):
    n = positions.size
    half = ROPE_HALF
    freqs = ROPE_THETA ** (-jnp.arange(half, dtype=F32) / half)
    zeros = jnp.zeros((MLA_NOPE,), F32)
    freq_row = jnp.concatenate([zeros, freqs, freqs, jnp.zeros((HEAD_PAD - MLA_NOPE - MLA_ROPE,), F32)])[None, :]
    pos = positions.reshape(n, 1)
    return pl.pallas_call(
        _rope_table_kernel,
        grid=(n // tm,),
        in_specs=[pl.BlockSpec((tm, 1), lambda i: (i, 0)), _const_spec((1, HEAD_PAD))],
        out_specs=[pl.BlockSpec((tm, HEAD_PAD), lambda i: (i, 0))] * 2,
        out_shape=[jax.ShapeDtypeStruct((n, HEAD_PAD), F32)] * 2,
        compiler_params=_params(1),
        name="rope_tables",
    )(pos, freq_row)


def _in_proj_kernel(x_ref, g_ref, w_ref, b_ref, gate_ref, conv_ref, lat_ref, sb_ref, *, chunk):
    h = _rms(x_ref[...], g_ref[...]).astype(BF16)

    def run(out_ref, col0, width, epilogue):
        for c in range(0, width, chunk):
            cw = min(chunk, width - c)
            acc = jnp.dot(h, w_ref[:, col0 + c:col0 + c + cw], preferred_element_type=F32)
            out_ref[:, c:c + cw] = epilogue(acc, c, cw).astype(out_ref.dtype)

    run(gate_ref, 0, GATE_COLS, lambda a, c, cw: jax.nn.sigmoid(a + b_ref[:, c:c + cw]))
    ident = lambda a, c, cw: a
    run(conv_ref, GATE_COLS, CONV_COLS, ident)
    run(lat_ref, GATE_COLS + CONV_COLS, LAT_COLS, ident)
    run(sb_ref, GATE_COLS + CONV_COLS + LAT_COLS, SB_COLS, ident)


def _in_proj(x2, gain, w_packed, b_gate, tm):
    n = x2.shape[0]
    total = w_packed.shape[1]
    widths = (GATE_COLS, CONV_COLS, LAT_COLS, SB_COLS)
    return pl.pallas_call(
        functools.partial(_in_proj_kernel, chunk=512),
        grid=(n // tm,),
        in_specs=[pl.BlockSpec((tm, D_MODEL), lambda i: (i, 0)), _const_spec((1, D_MODEL)),
                  _const_spec((D_MODEL, total)), _const_spec((1, GATE_COLS))],
        out_specs=[pl.BlockSpec((tm, w), lambda i: (i, 0)) for w in widths],
        out_shape=[jax.ShapeDtypeStruct((n, w), BF16) for w in widths],
        compiler_params=_params(1),
        name="in_proj",
    )(x2, gain, w_packed, b_gate)


def _pack_w_in(w):
    kr = w[:, OFF_KR:OFF_SB]
    x1, x2 = kr[:, :ROPE_HALF], kr[:, ROPE_HALF:]
    z_lo = jnp.zeros((D_MODEL, MLA_NOPE), w.dtype)
    z_hi = jnp.zeros((D_MODEL, HEAD_PAD - MLA_NOPE - MLA_ROPE), w.dtype)
    kr_main = jnp.concatenate([z_lo, x1, x2, z_hi], axis=1)
    kr_swap = jnp.concatenate([z_lo, -x2, x1, z_hi], axis=1)
    q_sb = w[:, OFF_SB:OFF_SB + SB_WIDTH] * (LOG2E * SB_HEAD_DIM ** -0.5)
    return jnp.concatenate([w[:, OFF_GATE:], w[:, OFF_CONV:OFF_CQ], w[:, OFF_CQ:OFF_KR], kr_main, kr_swap,
                            q_sb, w[:, OFF_SB + SB_WIDTH:OFF_GATE]], axis=1).astype(BF16)


def _mla_prep_kernel(lat_ref, cos_ref, sin_ref, qn_ref, kvn_ref, wqm_ref, wqs_ref, wk_ref, wv_ref,
                     q_ref, k_ref, v_ref, *, scale):
    cos, sin = cos_ref[...], sin_ref[...]
    cq = _rms(lat_ref[:, :MLA_Q_RANK].astype(F32), qn_ref[...]).astype(BF16)
    ckv = _rms(lat_ref[:, MLA_Q_RANK:MLA_Q_RANK + MLA_KV_RANK].astype(F32), kvn_ref[...]).astype(BF16)
    kr0 = MLA_Q_RANK + MLA_KV_RANK
    kr = (lat_ref[:, kr0:kr0 + HEAD_PAD].astype(F32) * cos
          + lat_ref[:, kr0 + HEAD_PAD:kr0 + 2 * HEAD_PAD].astype(F32) * sin)
    qm = jnp.dot(cq, wqm_ref[...], preferred_element_type=F32)
    qs = jnp.dot(cq, wqs_ref[...], preferred_element_type=F32)
    kn = jnp.dot(ckv, wk_ref[...], preferred_element_type=F32)
    for h in range(MLA_HEADS):
        sl = slice(h * HEAD_PAD, (h + 1) * HEAD_PAD)
        q_ref[:, sl] = ((qm[:, sl] * cos + qs[:, sl] * sin) * scale).astype(BF16)
        k_ref[:, sl] = (kn[:, sl] + kr).astype(BF16)
    v = jnp.dot(ckv, wv_ref[...], preferred_element_type=F32)
    lane = lax.broadcasted_iota(jnp.int32, v.shape, 1)
    v_ref[...] = jnp.where(lane % HEAD_PAD == MLA_V, 1.0, v).astype(BF16)


def _mla_prep(lat, cos, sin, q_norm, kv_norm, wq_main, wq_swap, wk, wv, tm):
    n = lat.shape[0]
    hw = MLA_HEADS * HEAD_PAD
    row = lambda w: pl.BlockSpec((tm, w), lambda i: (i, 0))
    return pl.pallas_call(
        functools.partial(_mla_prep_kernel, scale=LOG2E * (MLA_NOPE + MLA_ROPE) ** -0.5),
        grid=(n // tm,),
        in_specs=[row(LAT_COLS), row(HEAD_PAD), row(HEAD_PAD), _const_spec((1, MLA_Q_RANK)),
                  _const_spec((1, MLA_KV_RANK)), _const_spec((MLA_Q_RANK, hw)), _const_spec((MLA_Q_RANK, hw)),
                  _const_spec((MLA_KV_RANK, hw)), _const_spec((MLA_KV_RANK, hw))],
        out_specs=[row(hw), row(hw), row(hw)],
        out_shape=[jax.ShapeDtypeStruct((n, hw), BF16)] * 3,
        compiler_params=_params(1),
        name="mla_prep",
    )(lat, cos, sin, q_norm, kv_norm, wq_main, wq_swap, wk, wv)


def _pack_mla_weights(w_uq, w_ukv):
    qd = MLA_NOPE + MLA_ROPE
    z_hi = jnp.zeros((MLA_Q_RANK, HEAD_PAD - qd), w_uq.dtype)
    z_lo = jnp.zeros((MLA_Q_RANK, MLA_NOPE), w_uq.dtype)
    main, swap, wk, wv = [], [], [], []
    for h in range(MLA_HEADS):
        wq = w_uq[:, h * qd:(h + 1) * qd]
        nope, x1, x2 = wq[:, :MLA_NOPE], wq[:, MLA_NOPE:MLA_NOPE + ROPE_HALF], wq[:, MLA_NOPE + ROPE_HALF:]
        main += [nope, x1, x2, z_hi]
        swap += [z_lo, -x2, x1, z_hi]
        kv = w_ukv[:, h * (MLA_NOPE + MLA_V):(h + 1) * (MLA_NOPE + MLA_V)]
        wk += [kv[:, :MLA_NOPE], jnp.zeros((MLA_KV_RANK, HEAD_PAD - MLA_NOPE), w_ukv.dtype)]
        wv += [kv[:, MLA_NOPE:], jnp.zeros((MLA_KV_RANK, HEAD_PAD - MLA_V), w_ukv.dtype)]
    cat = lambda parts: jnp.concatenate(parts, axis=1).astype(BF16)
    return cat(main), cat(swap), cat(wk), cat(wv)


def _mla_attn_kernel(q_ref, k_ref, v_ref, o_ref, m_ref, acc_ref, *, tq):
    i = pl.program_id(1)
    row = lax.broadcasted_iota(jnp.int32, (tq, tq), 0)
    col = lax.broadcasted_iota(jnp.int32, (tq, tq), 1)
    causal = row >= col
    dn = (((1,), (1,)), ((), ()))

    def step(j, diagonal):
        start = pl.multiple_of(j * tq, tq)
        for h in range(MLA_HEADS):
            hs = slice(h * HEAD_PAD, (h + 1) * HEAD_PAD)
            s = lax.dot_general(q_ref[:, hs], k_ref[pl.ds(start, tq), hs], dn, preferred_element_type=F32)
            if diagonal:
                s = jnp.where(causal, s, -jnp.inf)
            m_cur = jnp.max(s, axis=1, keepdims=True)
            if diagonal:
                m_new = jnp.broadcast_to(m_cur, (tq, LANES))
            else:
                m_old = m_ref[h]
                m_new = jnp.maximum(m_old, m_cur)
            p = jnp.exp2(s - jnp.concatenate([m_new] * (tq // LANES), axis=1))
            pv = jnp.dot(p.astype(BF16), v_ref[pl.ds(start, tq), hs], preferred_element_type=F32)
            if diagonal:
                acc_ref[h] = pv
            else:
                acc_ref[h] = jnp.exp2(m_old - m_new) * acc_ref[h] + pv
            m_ref[h] = m_new

    step(i, True)

    def body(j, carry):
        step(j, False)
        return carry

    lax.fori_loop(0, i, body, 0)
    outs = []
    for h in range(MLA_HEADS):
        acc = acc_ref[h]
        outs.append(acc[:, :MLA_V] / acc[:, MLA_V:MLA_V + 1])
    o_ref[...] = jnp.concatenate(outs, axis=1).astype(o_ref.dtype)


def _mla_attn(q, k, v, batch, seq, tq):
    n = q.shape[0]
    hw = MLA_HEADS * HEAD_PAD
    vw = MLA_HEADS * MLA_V
    nq = seq // tq
    return pl.pallas_call(
        functools.partial(_mla_attn_kernel, tq=tq),
        grid=(batch, nq),
        in_specs=[pl.BlockSpec((tq, hw), lambda b, i: (b * nq + i, 0)),
                  pl.BlockSpec((seq, hw), lambda b, i: (b, 0)),
                  pl.BlockSpec((seq, hw), lambda b, i: (b, 0))],
        out_specs=pl.BlockSpec((tq, vw), lambda b, i: (b * nq + i, 0)),
        out_shape=jax.ShapeDtypeStruct((n, vw), BF16),
        scratch_shapes=[pltpu.VMEM((MLA_HEADS, tq, LANES), F32), pltpu.VMEM((MLA_HEADS, tq, HEAD_PAD), F32)],
        compiler_params=_params(2),
        name="mla_attn",
    )(q, k, v)


def _sb_attn_kernel(q_ref, k_ref, v_ref, tri_ref, o_ref, run_ref, acc_ref, *, tq):
    i = pl.program_id(1)
    row = lax.broadcasted_iota(jnp.int32, (tq, tq), 0)
    col = lax.broadcasted_iota(jnp.int32, (tq, tq), 1)
    strict = row > col
    dn = (((1,), (1,)), ((), ()))
    tri = tri_ref[...]
    sign = jnp.uint32(0x80000000)

    def step(j, diagonal):
        start = pl.multiple_of(j * tq, tq)
        for h in range(SB_HEADS):
            hs = slice(h * SB_HEAD_DIM, (h + 1) * SB_HEAD_DIM)
            z = lax.dot_general(q_ref[:, hs], k_ref[pl.ds(start, tq), hs], dn, preferred_element_type=F32)
            neg_abs = pltpu.bitcast(pltpu.bitcast(z, jnp.uint32) | sign, F32)
            sp = jnp.maximum(z, 0.0) + jnp.log(1.0 + jnp.exp2(neg_abs)) * LOG2E
            if diagonal:
                sp = jnp.where(strict, sp, 0.0)
            later = jnp.dot(sp.astype(BF16), tri, preferred_element_type=F32)
            a = jnp.exp2(z - sp - later)
            if diagonal:
                a = jnp.where(strict, a, 0.0)
            av = jnp.dot(a.astype(BF16), v_ref[pl.ds(start, tq), hs], preferred_element_type=F32)
            total = jnp.sum(sp, axis=1, keepdims=True)
            if diagonal:
                acc_ref[h] = av
                run_ref[h] = jnp.broadcast_to(total, (tq, LANES))
            else:
                run = run_ref[h]
                acc_ref[h] += jnp.exp2(-run[:, :SB_HEAD_DIM]) * av
                run_ref[h] = run + total

    step(i, True)

    def body(t, carry):
        step(i - 1 - t, False)
        return carry

    lax.fori_loop(0, i, body, 0)
    o_ref[...] = jnp.concatenate([acc_ref[h] for h in range(SB_HEADS)], axis=1).astype(o_ref.dtype)


def _sb_attn(qkv, batch, seq, tq):
    n = qkv.shape[0]
    nq = seq // tq
    j = jnp.arange(tq)
    tri = (j[:, None] > j[None, :]).astype(BF16)
    return pl.pallas_call(
        functools.partial(_sb_attn_kernel, tq=tq),
        grid=(batch, nq),
        in_specs=[pl.BlockSpec((tq, SB_WIDTH), lambda b, i: (b * nq + i, 0)),
                  pl.BlockSpec((seq, SB_WIDTH), lambda b, i: (b, 1)),
                  pl.BlockSpec((seq, SB_WIDTH), lambda b, i: (b, 2)),
                  _const_spec((tq, tq))],
        out_specs=pl.BlockSpec((tq, SB_WIDTH), lambda b, i: (b * nq + i, 0)),
        out_shape=jax.ShapeDtypeStruct((n, SB_WIDTH), BF16),
        scratch_shapes=[pltpu.VMEM((SB_HEADS, tq, LANES), F32), pltpu.VMEM((SB_HEADS, tq, SB_HEAD_DIM), F32)],
        compiler_params=_params(2),
        name="sb_attn",
    )(qkv, qkv, qkv, tri)


def _merge_kernel(x_ref, gate_ref, conv_ref, halo_ref, omla_ref, osb_ref, cw_ref, wa_ref, wb_ref, wc_ref,
                  wo_ref, fn_ref, wr_hi_ref, wr_lo_ref, br_ref, low_ref, xo_ref, h2_ref, route_ref, cnt_ref,
                  *, tm, tiles_per_seq):
    i = pl.program_id(0)
    f = lambda r: r.astype(F32)
    conv = conv_ref[...]
    u = f(conv[:, 2 * CONV_WIDTH:]) * f(conv[:, :CONV_WIDTH])
    halo = halo_ref[...]
    first = (i % tiles_per_seq) == 0
    up = f(halo[:, 2 * CONV_WIDTH:]) * f(halo[:, :CONV_WIDTH])
    up = jnp.where(first, 0.0, up)
    ue = jnp.concatenate([up, u], axis=0)
    cw = cw_ref[...]
    y = cw[0:1, :] * ue[6:tm + 6] + cw[1:2, :] * ue[7:tm + 7] + cw[2:3, :] * u
    ya = (f(conv[:, CONV_WIDTH:2 * CONV_WIDTH]) * y).astype(BF16)
    gate = gate_ref[...]
    merged = (f(gate[:, :D_MODEL]) * jnp.dot(ya, wa_ref[...], preferred_element_type=F32)
              + f(gate[:, D_MODEL:2 * D_MODEL]) * jnp.dot(omla_ref[...], wb_ref[...], preferred_element_type=F32)
              + f(gate[:, 2 * D_MODEL:]) * jnp.dot(osb_ref[...], wc_ref[...], preferred_element_type=F32))
    x_new = x_ref[...] + jnp.dot(merged.astype(BF16), wo_ref[...], preferred_element_type=F32)
    xo_ref[...] = x_new
    h2 = _rms(x_new, fn_ref[...])
    h2_hi = h2.astype(BF16)
    for c in range(ROW_TILES):
        h2_ref[pl.ds(c, tm, stride=ROW_TILES), :] = h2_hi[:, c * LANES:(c + 1) * LANES].astype(F32)
    h2_lo = (h2 - h2_hi.astype(F32)).astype(BF16)
    logits = (jnp.dot(h2_hi, wr_hi_ref[...], preferred_element_type=F32)
              + jnp.dot(h2_lo, wr_hi_ref[...], preferred_element_type=F32)
              + jnp.dot(h2_hi, wr_lo_ref[...], preferred_element_type=F32)) + br_ref[...]
    w1, w2, i1, i2 = _route(logits)

    @pl.when(i == 0)
    def _():
        cnt_ref[...] = jnp.zeros_like(cnt_ref)

    lane = lax.broadcasted_iota(jnp.int32, logits.shape, 1)
    pick1, pick2 = lane == i1, lane == i2
    onehot = jnp.where(pick1 | pick2, 1.0, 0.0)
    before = jnp.dot(low_ref[...], onehot.astype(BF16), preferred_element_type=F32) + cnt_ref[...]
    rank1 = jnp.sum(jnp.where(pick1, before, 0.0), axis=-1, keepdims=True)
    rank2 = jnp.sum(jnp.where(pick2, before, 0.0), axis=-1, keepdims=True)
    cnt_ref[...] += jnp.sum(onehot, axis=0, keepdims=True)
    cols = (w1, w2, i1.astype(F32), i2.astype(F32), rank1, rank2)
    route = jnp.zeros(logits.shape, F32)
    for k, col in enumerate(cols):
        route = jnp.where(lane == k, col, route)
    route_ref[...] = route


def _route(logits):
    lane = lax.broadcasted_iota(jnp.int32, logits.shape, 1)
    big = jnp.int32(1 << 30)
    neg = -jnp.inf
    is_group = (lane >= N_EXPERTS) & (lane < N_EXPERTS + N_GROUPS)
    gl = jnp.where(is_group, logits, neg)
    gmax = jnp.max(gl, axis=-1, keepdims=True)
    g_w = 1.0 / jnp.sum(jnp.exp(gl - gmax), axis=-1, keepdims=True)
    g_idx = jnp.min(jnp.where(gl == gmax, lane, big), axis=-1, keepdims=True) - N_EXPERTS
    lo = g_idx * EXPERTS_PER_GROUP
    el = jnp.where((lane >= lo) & (lane < lo + EXPERTS_PER_GROUP), logits, neg)
    m1 = jnp.max(el, axis=-1, keepdims=True)
    i1 = jnp.min(jnp.where(el == m1, lane, big), axis=-1, keepdims=True)
    el2 = jnp.where(lane == i1, neg, el)
    m2 = jnp.max(el2, axis=-1, keepdims=True)
    i2 = jnp.min(jnp.where(el2 == m2, lane, big), axis=-1, keepdims=True)
    r = jnp.exp(m2 - m1)
    w1 = g_w / (1.0 + r)
    w2 = g_w * r / (1.0 + r)
    return w1, w2, i1, i2


def _merge(x2, gates, conv, o_mla, o_sb, conv_w, wa, wb, wc, wo, ffn_norm, wr_hi, wr_lo, b_router, seq, tm):
    n = x2.shape[0]
    row = lambda w: pl.BlockSpec((tm, w), lambda i: (i, 0))
    sub = tm // 8
    halo = pl.BlockSpec((8, CONV_COLS), lambda i: (jnp.maximum(i * sub - 1, 0), 0))
    r = jnp.arange(tm)
    lower = (r[:, None] > r[None, :]).astype(BF16)
    return pl.pallas_call(
        functools.partial(_merge_kernel, tm=tm, tiles_per_seq=seq // tm),
        grid=(n // tm,),
        in_specs=[row(D_MODEL), row(GATE_COLS), row(CONV_COLS), halo, row(CONV_WIDTH), row(SB_WIDTH),
                  _const_spec((8, CONV_WIDTH)), _const_spec((CONV_WIDTH, D_MODEL)),
                  _const_spec((MLA_HEADS * MLA_V, D_MODEL)), _const_spec((SB_WIDTH, D_MODEL)),
                  _const_spec((D_MODEL, D_MODEL)), _const_spec((1, D_MODEL)),
                  _const_spec((D_MODEL, ROUTER_COLS)), _const_spec((D_MODEL, ROUTER_COLS)),
                  _const_spec((1, ROUTER_COLS)), _const_spec((tm, tm))],
        out_specs=[row(D_MODEL), pl.BlockSpec((tm * ROW_TILES, LANES), lambda i: (i, 0)), row(ROUTER_COLS),
                   _const_spec((1, ROUTER_COLS))],
        out_shape=[jax.ShapeDtypeStruct((n, D_MODEL), F32), jax.ShapeDtypeStruct((n * ROW_TILES, LANES), F32),
                   jax.ShapeDtypeStruct((n, ROUTER_COLS), F32), jax.ShapeDtypeStruct((1, ROUTER_COLS), F32)],
        compiler_params=_params(1),
        name="merge",
    )(x2, gates, conv, conv, o_mla, o_sb, conv_w, wa, wb, wc, wo, ffn_norm, wr_hi, wr_lo, b_router, lower)


def _pack_router(w_rg, b_rg, w_re, b_re):
    pad = ROUTER_COLS - N_EXPERTS - N_GROUPS
    w = jnp.concatenate([w_re, w_rg, jnp.zeros((D_MODEL, pad), F32)], axis=1)
    b = jnp.concatenate([b_re, b_rg, jnp.zeros((pad,), F32)])[None, :]
    hi = w.astype(BF16)
    lo = (w - hi.astype(F32)).astype(BF16)
    return hi, lo, b


def _token_rows(ref, first_token, count=1):
    start = pl.multiple_of(first_token * ROW_TILES, ROW_TILES)
    return ref.at[pl.ds(start, count * ROW_TILES)]


def _moe_plan(route, counts, n):
    cnt = counts[0, :N_EXPERTS].astype(jnp.int32)
    padded = (cnt + MOE_TILE - 1) // MOE_TILE * MOE_TILE
    ends = jnp.cumsum(padded)
    starts = ends - padded
    n_used = ends[-1] // MOE_TILE
    expert = route[:, 2:4].astype(jnp.int32)
    rank = route[:, 4:6].astype(jnp.int32)
    chosen = expert[..., None] == jnp.arange(N_EXPERTS, dtype=jnp.int32)
    pos = jnp.sum(jnp.where(chosen, starts, 0), axis=-1) + rank
    n_tiles = (2 * n) // MOE_TILE + N_EXPERTS
    tile_start = jnp.minimum(jnp.arange(n_tiles, dtype=jnp.int32), n_used - 1) * MOE_TILE
    tile_expert = jnp.sum((ends[None, :] <= tile_start[:, None]).astype(jnp.int32), axis=1)
    last_tile = jnp.where(padded > 0, ends // MOE_TILE - 1, -1)
    tail = n_used + jnp.arange(N_EXPERTS, dtype=jnp.int32)
    zero_tiles = jnp.concatenate([last_tile, jnp.where(tail < n_tiles, tail, -1)])
    return pos, tile_expert, n_used.reshape(1), zero_tiles, n_tiles


def _dispatch_kernel(zt_ref, pos_ref, h_hbm, xs_hbm, zero_ref, sem, zsem, *, td, n_steps):
    t = pl.program_id(0)

    def fill(j):
        return pltpu.make_async_copy(zero_ref, _token_rows(xs_hbm, zt_ref[j] * MOE_TILE, MOE_TILE), zsem)

    @pl.when(t == 0)
    def _():
        zero_ref[...] = jnp.zeros_like(zero_ref)
        for j in range(2 * N_EXPERTS):
            @pl.when(zt_ref[j] >= 0)
            def _():
                fill(j).start()
        for j in range(2 * N_EXPERTS):
            @pl.when(zt_ref[j] >= 0)
            def _():
                fill(j).wait()

    for r in range(td):
        src = _token_rows(h_hbm, t * td + r)
        for k in range(2):
            pltpu.make_async_copy(src, _token_rows(xs_hbm, pos_ref[0, 0, 2 * r + k]), sem).start()

    batch = pltpu.make_async_copy(_token_rows(h_hbm, 0, 2 * td), _token_rows(xs_hbm, 0, 2 * td), sem)

    @pl.when(t > 0)
    def _():
        batch.wait()

    @pl.when(t == n_steps - 1)
    def _():
        batch.wait()


def _dispatch(h2, pos3, zero_tiles, n_tiles, td):
    n = h2.shape[0] // ROW_TILES
    n_steps = n // td
    return pl.pallas_call(
        functools.partial(_dispatch_kernel, td=td, n_steps=n_steps),
        grid_spec=pltpu.PrefetchScalarGridSpec(
            num_scalar_prefetch=1, grid=(n_steps,),
            in_specs=[pl.BlockSpec((1, 1, 2 * td), lambda t, zt: (t, 0, 0), memory_space=pltpu.SMEM),
                      pl.BlockSpec(memory_space=pl.ANY)],
            out_specs=pl.BlockSpec(memory_space=pl.ANY),
            scratch_shapes=[pltpu.VMEM((MOE_TILE * ROW_TILES, LANES), F32), pltpu.SemaphoreType.DMA,
                            pltpu.SemaphoreType.DMA]),
        out_shape=jax.ShapeDtypeStruct((n_tiles * MOE_TILE * ROW_TILES, LANES), F32),
        compiler_params=_params(1),
        name="moe_dispatch",
    )(zero_tiles, pos3, h2)


def _experts_kernel(te_ref, nu_ref, xs_ref, wg_ref, wu_ref, wd_ref, y_ref):
    @pl.when(pl.program_id(0) >= nu_ref[0])
    def _():
        y_ref[...] = jnp.zeros_like(y_ref)

    @pl.when(pl.program_id(0) < nu_ref[0])
    def _():
        x = jnp.concatenate([xs_ref[pl.ds(c, MOE_TILE, stride=ROW_TILES), :] for c in range(ROW_TILES)],
                            axis=1).astype(BF16)
        g = jnp.dot(x, wg_ref[0], preferred_element_type=F32)
        u = jnp.dot(x, wu_ref[0], preferred_element_type=F32)
        act = (g * jax.nn.sigmoid(g) * u).astype(BF16)
        y = jnp.dot(act, wd_ref[0], preferred_element_type=F32)
        for c in range(ROW_TILES):
            y_ref[pl.ds(c, MOE_TILE, stride=ROW_TILES), :] = y[:, c * LANES:(c + 1) * LANES]


def _experts(xs, tile_expert, n_used, wg, wu, wd, n_tiles):
    shape = (MOE_TILE * ROW_TILES, LANES)
    rows_in = pl.BlockSpec(shape, lambda t, te, nu: (jnp.minimum(t, nu[0] - 1), 0))
    rows_out = pl.BlockSpec(shape, lambda t, te, nu: (t, 0))
    weight = lambda shape: pl.BlockSpec((1,) + shape, lambda t, te, nu: (te[t], 0, 0))
    return pl.pallas_call(
        _experts_kernel,
        grid_spec=pltpu.PrefetchScalarGridSpec(
            num_scalar_prefetch=2, grid=(n_tiles,),
            in_specs=[rows_in, weight((D_MODEL, D_EXPERT)), weight((D_MODEL, D_EXPERT)),
                      weight((D_EXPERT, D_MODEL))],
            out_specs=rows_out),
        out_shape=jax.ShapeDtypeStruct(xs.shape, F32),
        compiler_params=_params(1),
        name="moe_experts",
    )(tile_expert, n_used, xs, wg, wu, wd)


def _combine_kernel(pos_ref, posn_ref, x_ref, route_ref, fin_ref, y_hbm, o_ref, ybuf, sem, *, tc, n_steps, final):
    t = pl.program_id(0)

    def issue(p_ref, slot):
        for r in range(2 * tc):
            pltpu.make_async_copy(_token_rows(y_hbm, p_ref[0, 0, r]),
                                  ybuf.at[slot, pl.ds(r * ROW_TILES, ROW_TILES)], sem.at[slot]).start()

    def run(slot):
        if slot == 0:
            @pl.when(t == 0)
            def _():
                issue(pos_ref, 0)

        @pl.when(t + 1 < n_steps)
        def _():
            issue(posn_ref, 1 - slot)

        pltpu.make_async_copy(_token_rows(y_hbm, 0, 2 * tc), ybuf.at[slot], sem.at[slot]).wait()
        w1, w2 = route_ref[:, 0:1], route_ref[:, 1:2]
        outs = []
        for c in range(ROW_TILES):
            y1 = ybuf[slot, pl.ds(c, tc, stride=2 * ROW_TILES), :]
            y2 = ybuf[slot, pl.ds(ROW_TILES + c, tc, stride=2 * ROW_TILES), :]
            outs.append(x_ref[:, c * LANES:(c + 1) * LANES] + w1 * y1 + w2 * y2)
        out = jnp.concatenate(outs, axis=1)
        o_ref[...] = _rms(out, fin_ref[...]) if final else out

    for slot in range(2):
        @pl.when(t % 2 == slot)
        def _():
            run(slot)


def _combine(x_mid, route, y, pos3, final_norm, final, tc):
    n = x_mid.shape[0]
    n_steps = n // tc
    pos_spec = lambda step: pl.BlockSpec((1, 1, 2 * tc), lambda t: (step(t), 0, 0), memory_space=pltpu.SMEM)
    row = lambda w: pl.BlockSpec((tc, w), lambda t: (t, 0))
    return pl.pallas_call(
        functools.partial(_combine_kernel, tc=tc, n_steps=n_steps, final=final),
        grid=(n_steps,),
        in_specs=[pos_spec(lambda t: t), pos_spec(lambda t: jnp.minimum(t + 1, n_steps - 1)), row(D_MODEL),
                  row(ROUTER_COLS), _const_spec((1, D_MODEL)), pl.BlockSpec(memory_space=pl.ANY)],
        out_specs=row(D_MODEL),
        out_shape=jax.ShapeDtypeStruct((n, D_MODEL), F32),
        scratch_shapes=[pltpu.VMEM((2, 2 * tc * ROW_TILES, LANES), F32), pltpu.SemaphoreType.DMA((2,))],
        compiler_params=_params(1),
        name="moe_combine",
    )(pos3, pos3, x_mid, route, final_norm, y)


def _moe(x_mid, h2, route, counts, wg, wu, wd, final_norm, final):
    n = x_mid.shape[0]
    pos, tile_expert, n_used, zero_tiles, n_tiles = _moe_plan(route, counts, n)
    td = _tile(n, 256)
    pos3 = pos.reshape(n // td, 1, 2 * td)
    xs = _dispatch(h2, pos3, zero_tiles, n_tiles, td)
    y = _experts(xs, tile_expert, n_used, wg, wu, wd, n_tiles)
    return _combine(x_mid, route, y, pos3, final_norm, final, td)


def _tile(n, pref):
    t = min(n, pref)
    assert n % t == 0, (n, t)
    return t


def kernel(x, positions, attn_norm, w_in, b_gate, conv_w, w_out_conv, q_norm, kv_norm, w_uq, w_ukv, w_out_mla, w_out_sb, w_o, ffn_norm, w_router_group, b_router_group, w_router_expert, b_router_expert, w_exp_gate, w_exp_up, w_exp_down, final_norm):
    batch, seq, d = x.shape
    assert d == D_MODEL
    n = batch * seq
    depth = w_in.shape[0]
    tm_rows = _tile(seq, 512)
    tq = _tile(seq, 256)
    x2 = x.reshape(n, d)
    cos, sin = _rope_tables(positions, _tile(n, 1024))
    for l in range(depth):
        gates, conv, lat, sb = _in_proj(x2, attn_norm[l][None, :], _pack_w_in(w_in[l]), b_gate[l][None, :], tm_rows)
        wq_main, wq_swap, wk, wv = _pack_mla_weights(w_uq[l], w_ukv[l])
        q, k, v = _mla_prep(lat, cos, sin, q_norm[l][None, :], kv_norm[l][None, :], wq_main, wq_swap, wk, wv,
                            tm_rows)
        o_mla = _mla_attn(q, k, v, batch, seq, tq)
        o_sb = _sb_attn(sb, batch, seq, tq)
        wr_hi, wr_lo, b_router = _pack_router(w_router_group[l], b_router_group[l], w_router_expert[l],
                                              b_router_expert[l])
        conv_w8 = jnp.concatenate([conv_w[l], jnp.zeros((8 - CONV_K, CONV_WIDTH), F32)], axis=0)
        x_mid, h2, route, counts = _merge(x2, gates, conv, o_mla, o_sb, conv_w8, w_out_conv[l].astype(BF16),
                                          w_out_mla[l].astype(BF16), w_out_sb[l].astype(BF16),
                                          w_o[l].astype(BF16), ffn_norm[l][None, :], wr_hi, wr_lo, b_router, seq,
                                          tm_rows)
        x2 = _moe(x_mid, h2, route, counts, w_exp_gate[l].astype(BF16), w_exp_up[l].astype(BF16),
                  w_exp_down[l].astype(BF16), final_norm[None, :], l == depth - 1)
    return x2.reshape(batch, seq, d)
```

```python
import functools

import jax
import jax.numpy as jnp
from jax import lax
from jax.experimental import pallas as pl
from jax.experimental.pallas import tpu as pltpu

D_MODEL = 1024
CONV_WIDTH = 512
CONV_K = 3
MLA_HEADS = 8
MLA_NOPE = 64
MLA_ROPE = 32
MLA_V = 64
MLA_Q_RANK = 256
MLA_KV_RANK = 128
ROPE_THETA = 10000.0
SB_HEADS = 8
SB_HEAD_DIM = 64
SB_WIDTH = SB_HEADS * SB_HEAD_DIM
N_BRANCHES = 3
OFF_CONV = 0
OFF_CQ = OFF_CONV + 3 * CONV_WIDTH
OFF_CKV = OFF_CQ + MLA_Q_RANK
OFF_KR = OFF_CKV + MLA_KV_RANK
OFF_SB = OFF_KR + MLA_ROPE
OFF_GATE = OFF_SB + 3 * SB_WIDTH
N_GROUPS = 4
EXPERTS_PER_GROUP = 8
N_EXPERTS = N_GROUPS * EXPERTS_PER_GROUP
D_EXPERT = 256
EPS = 1e-6
LOG2E = 1.4426950408889634
SB_DEAD = 160.0

LANES = 128
HEAD_PAD = 128
ROPE_HALF = MLA_ROPE // 2
GATE_COLS = N_BRANCHES * D_MODEL
CONV_COLS = 3 * CONV_WIDTH
LAT_COLS = MLA_Q_RANK + MLA_KV_RANK + 2 * HEAD_PAD
SB_COLS = 3 * SB_WIDTH
ROUTER_COLS = LANES
ROW_TILES = D_MODEL // LANES
MOE_TILE = 256
VMEM_LIMIT = 56 * 1024 * 1024

BF16 = jnp.bfloat16
F32 = jnp.float32


def _params(n_axes, vmem=VMEM_LIMIT):
    return pltpu.CompilerParams(dimension_semantics=("arbitrary",) * n_axes, vmem_limit_bytes=vmem)


def _rms(xf, gain):
    return xf * lax.rsqrt(jnp.mean(xf * xf, axis=-1, keepdims=True) + EPS) * gain


def _const_spec(shape):
    return pl.BlockSpec(shape, lambda *_: (0,) * len(shape))


def _rope_table_kernel(pos_ref, freq_ref, cos_ref, sin_ref):
    ang = pos_ref[...].astype(F32) * freq_ref[...]
    cos_ref[...] = jnp.cos(ang)
    sin_ref[...] = jnp.sin(ang)


def _rope_tables(positions, tm):
    n = positions.size
    half = ROPE_HALF
    freqs = ROPE_THETA ** (-jnp.arange(half, dtype=F32) / half)
    zeros = jnp.zeros((MLA_NOPE,), F32)
    freq_row = jnp.concatenate([zeros, freqs, freqs, jnp.zeros((HEAD_PAD - MLA_NOPE - MLA_ROPE,), F32)])[None, :]
    pos = positions.reshape(n, 1)
    return pl.pallas_call(
        _rope_table_kernel,
        grid=(n // tm,),
        in_specs=[pl.BlockSpec((tm, 1), lambda i: (i, 0)), _const_spec((1, HEAD_PAD))],
        out_specs=[pl.BlockSpec((tm, HEAD_PAD), lambda i: (i, 0))] * 2,
        out_shape=[jax.ShapeDtypeStruct((n, HEAD_PAD), F32)] * 2,
        compiler_params=_params(1),
        name="rope_tables",
    )(pos, freq_row)


def _in_proj_kernel(x_ref, g_ref, w_ref, b_ref, gate_ref, conv_ref, lat_ref, sb_ref, *, chunk):
    h = _rms(x_ref[...], g_ref[...]).astype(BF16)

    def run(out_ref, col0, width, epilogue):
        for c in range(0, width, chunk):
            cw = min(chunk, width - c)
            acc = jnp.dot(h, w_ref[:, col0 + c:col0 + c + cw], preferred_element_type=F32)
            out_ref[:, c:c + cw] = epilogue(acc, c, cw).astype(out_ref.dtype)

    run(gate_ref, 0, GATE_COLS, lambda a, c, cw: jax.nn.sigmoid(a + b_ref[:, c:c + cw]))
    ident = lambda a, c, cw: a
    run(conv_ref, GATE_COLS, CONV_COLS, ident)
    run(lat_ref, GATE_COLS + CONV_COLS, LAT_COLS, ident)
    run(sb_ref, GATE_COLS + CONV_COLS + LAT_COLS, SB_COLS, ident)


def _in_proj(x2, gain, w_packed, b_gate, tm):
    n = x2.shape[0]
    total = w_packed.shape[1]
    widths = (GATE_COLS, CONV_COLS, LAT_COLS, SB_COLS)
    return pl.pallas_call(
        functools.partial(_in_proj_kernel, chunk=512),
        grid=(n // tm,),
        in_specs=[pl.BlockSpec((tm, D_MODEL), lambda i: (i, 0)), _const_spec((1, D_MODEL)),
                  _const_spec((D_MODEL, total)), _const_spec((1, GATE_COLS))],
        out_specs=[pl.BlockSpec((tm, w), lambda i: (i, 0)) for w in widths],
        out_shape=[jax.ShapeDtypeStruct((n, w), BF16) for w in widths],
        compiler_params=_params(1),
        name="in_proj",
    )(x2, gain, w_packed, b_gate)


def _pack_w_in(w):
    kr = w[:, OFF_KR:OFF_SB]
    x1, x2 = kr[:, :ROPE_HALF], kr[:, ROPE_HALF:]
    z_lo = jnp.zeros((D_MODEL, MLA_NOPE), w.dtype)
    z_hi = jnp.zeros((D_MODEL, HEAD_PAD - MLA_NOPE - MLA_ROPE), w.dtype)
    kr_main = jnp.concatenate([z_lo, x1, x2, z_hi], axis=1)
    kr_swap = jnp.concatenate([z_lo, -x2, x1, z_hi], axis=1)
    q_sb = w[:, OFF_SB:OFF_SB + SB_WIDTH] * (LOG2E * SB_HEAD_DIM ** -0.5)
    return jnp.concatenate([w[:, OFF_GATE:], w[:, OFF_CONV:OFF_CQ], w[:, OFF_CQ:OFF_KR], kr_main, kr_swap,
                            q_sb, w[:, OFF_SB + SB_WIDTH:OFF_GATE]], axis=1).astype(BF16)


def _mla_prep_kernel(lat_ref, cos_ref, sin_ref, qn_ref, kvn_ref, wqm_ref, wqs_ref, wk_ref, wv_ref,
                     q_ref, k_ref, v_ref, *, scale):
    cos, sin = cos_ref[...], sin_ref[...]
    cq = _rms(lat_ref[:, :MLA_Q_RANK].astype(F32), qn_ref[...]).astype(BF16)
    ckv = _rms(lat_ref[:, MLA_Q_RANK:MLA_Q_RANK + MLA_KV_RANK].astype(F32), kvn_ref[...]).astype(BF16)
    kr0 = MLA_Q_RANK + MLA_KV_RANK
    kr = (lat_ref[:, kr0:kr0 + HEAD_PAD].astype(F32) * cos
          + lat_ref[:, kr0 + HEAD_PAD:kr0 + 2 * HEAD_PAD].astype(F32) * sin)
    qm = jnp.dot(cq, wqm_ref[...], preferred_element_type=F32)
    qs = jnp.dot(cq, wqs_ref[...], preferred_element_type=F32)
    kn = jnp.dot(ckv, wk_ref[...], preferred_element_type=F32)
    for h in range(MLA_HEADS):
        sl = slice(h * HEAD_PAD, (h + 1) * HEAD_PAD)
        q_ref[:, sl] = ((qm[:, sl] * cos + qs[:, sl] * sin) * scale).astype(BF16)
        k_ref[:, sl] = (kn[:, sl] + kr).astype(BF16)
    v = jnp.dot(ckv, wv_ref[...], preferred_element_type=F32)
    lane = lax.broadcasted_iota(jnp.int32, v.shape, 1)
    v_ref[...] = jnp.where(lane % HEAD_PAD == MLA_V, 1.0, v).astype(BF16)


def _mla_prep(lat, cos, sin, q_norm, kv_norm, wq_main, wq_swap, wk, wv, tm):
    n = lat.shape[0]
    hw = MLA_HEADS * HEAD_PAD
    row = lambda w: pl.BlockSpec((tm, w), lambda i: (i, 0))
    return pl.pallas_call(
        functools.partial(_mla_prep_kernel, scale=LOG2E * (MLA_NOPE + MLA_ROPE) ** -0.5),
        grid=(n // tm,),
        in_specs=[row(LAT_COLS), row(HEAD_PAD), row(HEAD_PAD), _const_spec((1, MLA_Q_RANK)),
                  _const_spec((1, MLA_KV_RANK)), _const_spec((MLA_Q_RANK, hw)), _const_spec((MLA_Q_RANK, hw)),
                  _const_spec((MLA_KV_RANK, hw)), _const_spec((MLA_KV_RANK, hw))],
        out_specs=[row(hw), row(hw), row(hw)],
        out_shape=[jax.ShapeDtypeStruct((n, hw), BF16)] * 3,
        compiler_params=_params(1),
        name="mla_prep",
    )(lat, cos, sin, q_norm, kv_norm, wq_main, wq_swap, wk, wv)


def _pack_mla_weights(w_uq, w_ukv):
    qd = MLA_NOPE + MLA_ROPE
    z_hi = jnp.zeros((MLA_Q_RANK, HEAD_PAD - qd), w_uq.dtype)
    z_lo = jnp.zeros((MLA_Q_RANK, MLA_NOPE), w_uq.dtype)
    main, swap, wk, wv = [], [], [], []
    for h in range(MLA_HEADS):
        wq = w_uq[:, h * qd:(h + 1) * qd]
        nope, x1, x2 = wq[:, :MLA_NOPE], wq[:, MLA_NOPE:MLA_NOPE + ROPE_HALF], wq[:, MLA_NOPE + ROPE_HALF:]
        main += [nope, x1, x2, z_hi]
        swap += [z_lo, -x2, x1, z_hi]
        kv = w_ukv[:, h * (MLA_NOPE + MLA_V):(h + 1) * (MLA_NOPE + MLA_V)]
        wk += [kv[:, :MLA_NOPE], jnp.zeros((MLA_KV_RANK, HEAD_PAD - MLA_NOPE), w_ukv.dtype)]
        wv += [kv[:, MLA_NOPE:], jnp.zeros((MLA_KV_RANK, HEAD_PAD - MLA_V), w_ukv.dtype)]
    cat = lambda parts: jnp.concatenate(parts, axis=1).astype(BF16)
    return cat(main), cat(swap), cat(wk), cat(wv)


def _mla_attn_kernel(q_ref, k_ref, v_ref, o_ref, m_ref, acc_ref, *, tq):
    i = pl.program_id(1)
    row = lax.broadcasted_iota(jnp.int32, (tq, tq), 0)
    col = lax.broadcasted_iota(jnp.int32, (tq, tq), 1)
    causal = row >= col
    dn = (((1,), (1,)), ((), ()))

    def step(j, width, diagonal=False):
        start = pl.multiple_of(j * tq, tq)
        for h in range(MLA_HEADS):
            hs = slice(h * HEAD_PAD, (h + 1) * HEAD_PAD)
            s = lax.dot_general(q_ref[:, hs], k_ref[pl.ds(start, width), hs], dn, preferred_element_type=F32)
            if diagonal:
                s = jnp.where(causal, s, -jnp.inf)
            m_cur = jnp.max(s, axis=1, keepdims=True)
            if diagonal:
                m_new = jnp.broadcast_to(m_cur, (tq, LANES))
            else:
                m_old = m_ref[h]
                m_new = jnp.maximum(m_old, m_cur)
            p = jnp.exp2(s - jnp.concatenate([m_new] * (width // LANES), axis=1))
            pv = jnp.dot(p.astype(BF16), v_ref[pl.ds(start, width), hs], preferred_element_type=F32)
            if diagonal:
                acc_ref[h] = pv
            else:
                acc_ref[h] = jnp.exp2(m_old - m_new) * acc_ref[h] + pv
            m_ref[h] = m_new

    step(i, tq, diagonal=True)

    unroll = 4

    def body(jj, carry):
        for u in range(unroll):
            step(unroll * jj + u, tq)
        return carry

    lax.fori_loop(0, i // unroll, body, 0)

    rest = i - i % unroll

    @pl.when(i % unroll >= 2)
    def _():
        step(rest, tq)
        step(rest + 1, tq)

    @pl.when(i % 2 == 1)
    def _():
        step(i - 1, tq)

    outs = []
    for h in range(MLA_HEADS):
        acc = acc_ref[h]
        outs.append(acc[:, :MLA_V] / acc[:, MLA_V:MLA_V + 1])
    o_ref[...] = jnp.concatenate(outs, axis=1).astype(o_ref.dtype)


def _mla_attn(q, k, v, batch, seq, tq):
    n = q.shape[0]
    hw = MLA_HEADS * HEAD_PAD
    vw = MLA_HEADS * MLA_V
    nq = seq // tq
    return pl.pallas_call(
        functools.partial(_mla_attn_kernel, tq=tq),
        grid=(batch, nq),
        in_specs=[pl.BlockSpec((tq, hw), lambda b, i: (b * nq + i, 0)),
                  pl.BlockSpec((seq, hw), lambda b, i: (b, 0)),
                  pl.BlockSpec((seq, hw), lambda b, i: (b, 0))],
        out_specs=pl.BlockSpec((tq, vw), lambda b, i: (b * nq + i, 0)),
        out_shape=jax.ShapeDtypeStruct((n, vw), BF16),
        scratch_shapes=[pltpu.VMEM((MLA_HEADS, tq, LANES), F32), pltpu.VMEM((MLA_HEADS, tq, HEAD_PAD), F32)],
        compiler_params=_params(2),
        name="mla_attn",
    )(q, k, v)


def _sb_attn_kernel(q_ref, k_ref, v_ref, tri_ref, o_ref, run_ref, acc_ref, *, tq):
    i = pl.program_id(1)
    row = lax.broadcasted_iota(jnp.int32, (tq, tq), 0)
    col = lax.broadcasted_iota(jnp.int32, (tq, tq), 1)
    strict = row > col
    dn = (((1,), (1,)), ((), ()))
    tri = tri_ref[...]

    def step(j, diagonal):
        start = pl.multiple_of(j * tq, tq)
        for h in range(SB_HEADS):
            hs = slice(h * SB_HEAD_DIM, (h + 1) * SB_HEAD_DIM)
            z = lax.dot_general(q_ref[:, hs], k_ref[pl.ds(start, tq), hs], dn, preferred_element_type=F32)
            sp = jnp.maximum(z, 0.0) + jnp.log(1.0 + jnp.exp2(-jnp.abs(z))) * LOG2E
            if diagonal:
                sp = jnp.where(strict, sp, 0.0)
            later = jnp.dot(sp.astype(BF16), tri, preferred_element_type=F32)
            a = jnp.exp2(z - sp - later)
            if diagonal:
                a = jnp.where(strict, a, 0.0)
            av = jnp.dot(a.astype(BF16), v_ref[pl.ds(start, tq), hs], preferred_element_type=F32)
            total = jnp.sum(sp, axis=1, keepdims=True)
            if diagonal:
                acc_ref[h] = av
                run_ref[h] = jnp.broadcast_to(total, (tq, LANES))
            else:
                run = run_ref[h]
                acc_ref[h] += jnp.exp2(-run[:, :SB_HEAD_DIM]) * av
                run_ref[h] = run + total

    @pl.when(i == 0)
    def _():
        step(0, True)

    @pl.when(i > 0)
    def _():
        step(i, True)
        step(i - 1, False)

    def alive():
        return (jnp.min(run_ref[...]) < SB_DEAD).astype(jnp.int32)

    def cond(carry):
        t, live = carry
        return (t < i) & (live > 0)

    def body(carry):
        t, _ = carry
        step(i - 1 - t, False)
        return t + 1, alive()

    lax.while_loop(cond, body, (jnp.int32(1), alive()))
    o_ref[...] = jnp.concatenate([acc_ref[h] for h in range(SB_HEADS)], axis=1).astype(o_ref.dtype)


def _sb_attn(qkv, batch, seq, tq):
    n = qkv.shape[0]
    nq = seq // tq
    j = jnp.arange(tq)
    tri = (j[:, None] > j[None, :]).astype(BF16)
    return pl.pallas_call(
        functools.partial(_sb_attn_kernel, tq=tq),
        grid=(batch, nq),
        in_specs=[pl.BlockSpec((tq, SB_WIDTH), lambda b, i: (b * nq + i, 0)),
                  pl.BlockSpec((seq, SB_WIDTH), lambda b, i: (b, 1)),
                  pl.BlockSpec((seq, SB_WIDTH), lambda b, i: (b, 2)),
                  _const_spec((tq, tq))],
        out_specs=pl.BlockSpec((tq, SB_WIDTH), lambda b, i: (b * nq + i, 0)),
        out_shape=jax.ShapeDtypeStruct((n, SB_WIDTH), BF16),
        scratch_shapes=[pltpu.VMEM((SB_HEADS, tq, LANES), F32), pltpu.VMEM((SB_HEADS, tq, SB_HEAD_DIM), F32)],
        compiler_params=_params(2),
        name="sb_attn",
    )(qkv, qkv, qkv, tri)


def _merge_kernel(x_ref, gate_ref, conv_ref, halo_ref, omla_ref, osb_ref, cw_ref, wa_ref, wb_ref, wc_ref,
                  wo_ref, fn_ref, wr_hi_ref, wr_lo_ref, br_ref, low_ref, xo_ref, h2_ref, route_ref, cnt_ref,
                  *, tm, tiles_per_seq):
    i = pl.program_id(0)
    f = lambda r: r.astype(F32)
    conv = conv_ref[...]
    u = f(conv[:, 2 * CONV_WIDTH:]) * f(conv[:, :CONV_WIDTH])
    halo = halo_ref[...]
    first = (i % tiles_per_seq) == 0
    up = f(halo[:, 2 * CONV_WIDTH:]) * f(halo[:, :CONV_WIDTH])
    up = jnp.where(first, 0.0, up)
    ue = jnp.concatenate([up, u], axis=0)
    cw = cw_ref[...]
    y = cw[0:1, :] * ue[6:tm + 6] + cw[1:2, :] * ue[7:tm + 7] + cw[2:3, :] * u
    ya = (f(conv[:, CONV_WIDTH:2 * CONV_WIDTH]) * y).astype(BF16)
    gate = gate_ref[...]
    merged = (f(gate[:, :D_MODEL]) * jnp.dot(ya, wa_ref[...], preferred_element_type=F32)
              + f(gate[:, D_MODEL:2 * D_MODEL]) * jnp.dot(omla_ref[...], wb_ref[...], preferred_element_type=F32)
              + f(gate[:, 2 * D_MODEL:]) * jnp.dot(osb_ref[...], wc_ref[...], preferred_element_type=F32))
    x_new = x_ref[...] + jnp.dot(merged.astype(BF16), wo_ref[...], preferred_element_type=F32)
    xo_ref[...] = x_new
    h2 = _rms(x_new, fn_ref[...])
    h2_hi = h2.astype(BF16)
    for c in range(ROW_TILES):
        h2_ref[pl.ds(c, tm, stride=ROW_TILES), :] = h2_hi[:, c * LANES:(c + 1) * LANES].astype(F32)
    h2_lo = (h2 - h2_hi.astype(F32)).astype(BF16)
    logits = (jnp.dot(h2_hi, wr_hi_ref[...], preferred_element_type=F32)
              + jnp.dot(h2_lo, wr_hi_ref[...], preferred_element_type=F32)
              + jnp.dot(h2_hi, wr_lo_ref[...], preferred_element_type=F32)) + br_ref[...]
    w1, w2, i1, i2 = _route(logits)

    @pl.when(i == 0)
    def _():
        cnt_ref[...] = jnp.zeros_like(cnt_ref)

    lane = lax.broadcasted_iota(jnp.int32, logits.shape, 1)
    pick1, pick2 = lane == i1, lane == i2
    onehot = jnp.where(pick1 | pick2, 1.0, 0.0)
    before = jnp.dot(low_ref[...], onehot.astype(BF16), preferred_element_type=F32) + cnt_ref[...]
    rank1 = jnp.sum(jnp.where(pick1, before, 0.0), axis=-1, keepdims=True)
    rank2 = jnp.sum(jnp.where(pick2, before, 0.0), axis=-1, keepdims=True)
    cnt_ref[...] += jnp.sum(onehot, axis=0, keepdims=True)
    cols = (w1, w2, i1.astype(F32), i2.astype(F32), rank1, rank2)
    route = jnp.zeros(logits.shape, F32)
    for k, col in enumerate(cols):
        route = jnp.where(lane == k, col, route)
    route_ref[...] = route


def _route(logits):
    lane = lax.broadcasted_iota(jnp.int32, logits.shape, 1)
    big = jnp.int32(1 << 30)
    neg = -jnp.inf
    is_group = (lane >= N_EXPERTS) & (lane < N_EXPERTS + N_GROUPS)
    gl = jnp.where(is_group, logits, neg)
    gmax = jnp.max(gl, axis=-1, keepdims=True)
    g_w = 1.0 / jnp.sum(jnp.exp(gl - gmax), axis=-1, keepdims=True)
    g_idx = jnp.min(jnp.where(gl == gmax, lane, big), axis=-1, keepdims=True) - N_EXPERTS
    lo = g_idx * EXPERTS_PER_GROUP
    el = jnp.where((lane >= lo) & (lane < lo + EXPERTS_PER_GROUP), logits, neg)
    m1 = jnp.max(el, axis=-1, keepdims=True)
    i1 = jnp.min(jnp.where(el == m1, lane, big), axis=-1, keepdims=True)
    el2 = jnp.where(lane == i1, neg, el)
    m2 = jnp.max(el2, axis=-1, keepdims=True)
    i2 = jnp.min(jnp.where(el2 == m2, lane, big), axis=-1, keepdims=True)
    r = jnp.exp(m2 - m1)
    w1 = g_w / (1.0 + r)
    w2 = g_w * r / (1.0 + r)
    return w1, w2, i1, i2


def _merge(x2, gates, conv, o_mla, o_sb, conv_w, wa, wb, wc, wo, ffn_norm, wr_hi, wr_lo, b_router, seq, tm):
    n = x2.shape[0]
    row = lambda w: pl.BlockSpec((tm, w), lambda i: (i, 0))
    sub = tm // 8
    halo = pl.BlockSpec((8, CONV_COLS), lambda i: (jnp.maximum(i * sub - 1, 0), 0))
    r = jnp.arange(tm)
    lower = (r[:, None] > r[None, :]).astype(BF16)
    return pl.pallas_call(
        functools.partial(_merge_kernel, tm=tm, tiles_per_seq=seq // tm),
        grid=(n // tm,),
        in_specs=[row(D_MODEL), row(GATE_COLS), row(CONV_COLS), halo, row(CONV_WIDTH), row(SB_WIDTH),
                  _const_spec((8, CONV_WIDTH)), _const_spec((CONV_WIDTH, D_MODEL)),
                  _const_spec((MLA_HEADS * MLA_V, D_MODEL)), _const_spec((SB_WIDTH, D_MODEL)),
                  _const_spec((D_MODEL, D_MODEL)), _const_spec((1, D_MODEL)),
                  _const_spec((D_MODEL, ROUTER_COLS)), _const_spec((D_MODEL, ROUTER_COLS)),
                  _const_spec((1, ROUTER_COLS)), _const_spec((tm, tm))],
        out_specs=[row(D_MODEL), pl.BlockSpec((tm * ROW_TILES, LANES), lambda i: (i, 0)), row(ROUTER_COLS),
                   _const_spec((1, ROUTER_COLS))],
        out_shape=[jax.ShapeDtypeStruct((n, D_MODEL), F32), jax.ShapeDtypeStruct((n * ROW_TILES, LANES), F32),
                   jax.ShapeDtypeStruct((n, ROUTER_COLS), F32), jax.ShapeDtypeStruct((1, ROUTER_COLS), F32)],
        compiler_params=_params(1),
        name="merge",
    )(x2, gates, conv, conv, o_mla, o_sb, conv_w, wa, wb, wc, wo, ffn_norm, wr_hi, wr_lo, b_router, lower)


def _pack_router(w_rg, b_rg, w_re, b_re):
    pad = ROUTER_COLS - N_EXPERTS - N_GROUPS
    w = jnp.concatenate([w_re, w_rg, jnp.zeros((D_MODEL, pad), F32)], axis=1)
    b = jnp.concatenate([b_re, b_rg, jnp.zeros((pad,), F32)])[None, :]
    hi = w.astype(BF16)
    lo = (w - hi.astype(F32)).astype(BF16)
    return hi, lo, b


def _token_rows(ref, first_token, count=1):
    start = pl.multiple_of(first_token * ROW_TILES, ROW_TILES)
    return ref.at[pl.ds(start, count * ROW_TILES)]


def _moe_plan(route, counts, n):
    cnt = counts[0, :N_EXPERTS].astype(jnp.int32)
    padded = (cnt + MOE_TILE - 1) // MOE_TILE * MOE_TILE
    ends = jnp.cumsum(padded)
    starts = ends - padded
    n_used = ends[-1] // MOE_TILE
    expert = route[:, 2:4].astype(jnp.int32)
    rank = route[:, 4:6].astype(jnp.int32)
    chosen = expert[..., None] == jnp.arange(N_EXPERTS, dtype=jnp.int32)
    pos = jnp.sum(jnp.where(chosen, starts, 0), axis=-1) + rank
    n_tiles = (2 * n) // MOE_TILE + N_EXPERTS
    tile_start = jnp.minimum(jnp.arange(n_tiles, dtype=jnp.int32), n_used - 1) * MOE_TILE
    tile_expert = jnp.sum((ends[None, :] <= tile_start[:, None]).astype(jnp.int32), axis=1)
    last_tile = jnp.where(padded > 0, ends // MOE_TILE - 1, -1)
    tail = n_used + jnp.arange(N_EXPERTS, dtype=jnp.int32)
    zero_tiles = jnp.concatenate([last_tile, jnp.where(tail < n_tiles, tail, -1)])
    return pos, tile_expert, n_used.reshape(1), zero_tiles, n_tiles


def _dispatch_kernel(zt_ref, pos_ref, h_ref, xs_hbm, zero_ref, sem, zsem, *, td):
    t = pl.program_id(0)

    def fill(j):
        return pltpu.make_async_copy(zero_ref, _token_rows(xs_hbm, zt_ref[j] * MOE_TILE, MOE_TILE), zsem)

    @pl.when(t == 0)
    def _():
        zero_ref[...] = jnp.zeros_like(zero_ref)
        for j in range(2 * N_EXPERTS):
            @pl.when(zt_ref[j] >= 0)
            def _():
                fill(j).start()
        for j in range(2 * N_EXPERTS):
            @pl.when(zt_ref[j] >= 0)
            def _():
                fill(j).wait()

    for r in range(td):
        src = h_ref.at[pl.ds(r * ROW_TILES, ROW_TILES)]
        for k in range(2):
            pltpu.make_async_copy(src, _token_rows(xs_hbm, pos_ref[0, 0, 2 * r + k]), sem).start()

    for k in range(2):
        pltpu.make_async_copy(h_ref, _token_rows(xs_hbm, 0, td), sem).wait()


def _dispatch(h2, pos3, zero_tiles, n_tiles, td):
    n = h2.shape[0] // ROW_TILES
    n_steps = n // td
    return pl.pallas_call(
        functools.partial(_dispatch_kernel, td=td),
        grid_spec=pltpu.PrefetchScalarGridSpec(
            num_scalar_prefetch=1, grid=(n_steps,),
            in_specs=[pl.BlockSpec((1, 1, 2 * td), lambda t, zt: (t, 0, 0), memory_space=pltpu.SMEM),
                      pl.BlockSpec((td * ROW_TILES, LANES), lambda t, zt: (t, 0))],
            out_specs=pl.BlockSpec(memory_space=pl.ANY),
            scratch_shapes=[pltpu.VMEM((MOE_TILE * ROW_TILES, LANES), F32), pltpu.SemaphoreType.DMA,
                            pltpu.SemaphoreType.DMA]),
        out_shape=jax.ShapeDtypeStruct((n_tiles * MOE_TILE * ROW_TILES, LANES), F32),
        compiler_params=_params(1),
        name="moe_dispatch",
    )(zero_tiles, pos3, h2)


def _experts_kernel(te_ref, nu_ref, xs_ref, wg_ref, wu_ref, wd_ref, y_ref):
    @pl.when(pl.program_id(0) >= nu_ref[0])
    def _():
        y_ref[...] = jnp.zeros_like(y_ref)

    @pl.when(pl.program_id(0) < nu_ref[0])
    def _():
        x = jnp.concatenate([xs_ref[pl.ds(c, MOE_TILE, stride=ROW_TILES), :] for c in range(ROW_TILES)],
                            axis=1).astype(BF16)
        g = jnp.dot(x, wg_ref[0], preferred_element_type=F32)
        u = jnp.dot(x, wu_ref[0], preferred_element_type=F32)
        act = (g * jax.nn.sigmoid(g) * u).astype(BF16)
        y = jnp.dot(act, wd_ref[0], preferred_element_type=F32)
        for c in range(ROW_TILES):
            y_ref[pl.ds(c, MOE_TILE, stride=ROW_TILES), :] = y[:, c * LANES:(c + 1) * LANES]


def _experts(xs, tile_expert, n_used, wg, wu, wd, n_tiles):
    shape = (MOE_TILE * ROW_TILES, LANES)
    rows_in = pl.BlockSpec(shape, lambda t, te, nu: (jnp.minimum(t, nu[0] - 1), 0))
    rows_out = pl.BlockSpec(shape, lambda t, te, nu: (t, 0))
    weight = lambda shape: pl.BlockSpec((1,) + shape, lambda t, te, nu: (te[t], 0, 0))
    return pl.pallas_call(
        _experts_kernel,
        grid_spec=pltpu.PrefetchScalarGridSpec(
            num_scalar_prefetch=2, grid=(n_tiles,),
            in_specs=[rows_in, weight((D_MODEL, D_EXPERT)), weight((D_MODEL, D_EXPERT)),
                      weight((D_EXPERT, D_MODEL))],
            out_specs=rows_out),
        out_shape=jax.ShapeDtypeStruct(xs.shape, F32),
        compiler_params=_params(1),
        name="moe_experts",
    )(tile_expert, n_used, xs, wg, wu, wd)


def _combine_kernel(pos_ref, posn_ref, x_ref, route_ref, fin_ref, y_hbm, o_ref, ybuf, sem, *, tc, n_steps, final):
    t = pl.program_id(0)

    def issue(p_ref, slot):
        for r in range(2 * tc):
            pltpu.make_async_copy(_token_rows(y_hbm, p_ref[0, 0, r]),
                                  ybuf.at[slot, pl.ds(r * ROW_TILES, ROW_TILES)], sem.at[slot]).start()

    def run(slot):
        if slot == 0:
            @pl.when(t == 0)
            def _():
                issue(pos_ref, 0)

        @pl.when(t + 1 < n_steps)
        def _():
            issue(posn_ref, 1 - slot)

        pltpu.make_async_copy(_token_rows(y_hbm, 0, 2 * tc), ybuf.at[slot], sem.at[slot]).wait()
        w1, w2 = route_ref[:, 0:1], route_ref[:, 1:2]
        outs = []
        for c in range(ROW_TILES):
            y1 = ybuf[slot, pl.ds(c, tc, stride=2 * ROW_TILES), :]
            y2 = ybuf[slot, pl.ds(ROW_TILES + c, tc, stride=2 * ROW_TILES), :]
            outs.append(x_ref[:, c * LANES:(c + 1) * LANES] + w1 * y1 + w2 * y2)
        out = jnp.concatenate(outs, axis=1)
        o_ref[...] = _rms(out, fin_ref[...]) if final else out

    for slot in range(2):
        @pl.when(t % 2 == slot)
        def _():
            run(slot)


def _combine(x_mid, route, y, pos3, final_norm, final, tc):
    n = x_mid.shape[0]
    n_steps = n // tc
    pos_spec = lambda step: pl.BlockSpec((1, 1, 2 * tc), lambda t: (step(t), 0, 0), memory_space=pltpu.SMEM)
    row = lambda w: pl.BlockSpec((tc, w), lambda t: (t, 0))
    return pl.pallas_call(
        functools.partial(_combine_kernel, tc=tc, n_steps=n_steps, final=final),
        grid=(n_steps,),
        in_specs=[pos_spec(lambda t: t), pos_spec(lambda t: jnp.minimum(t + 1, n_steps - 1)), row(D_MODEL),
                  row(ROUTER_COLS), _const_spec((1, D_MODEL)), pl.BlockSpec(memory_space=pl.ANY)],
        out_specs=row(D_MODEL),
        out_shape=jax.ShapeDtypeStruct((n, D_MODEL), F32),
        scratch_shapes=[pltpu.VMEM((2, 2 * tc * ROW_TILES, LANES), F32), pltpu.SemaphoreType.DMA((2,))],
        compiler_params=_params(1),
        name="moe_combine",
    )(pos3, pos3, x_mid, route, final_norm, y)


def _moe(x_mid, h2, route, counts, wg, wu, wd, final_norm, final):
    n = x_mid.shape[0]
    pos, tile_expert, n_used, zero_tiles, n_tiles = _moe_plan(route, counts, n)
    td, tc = _tile(n, 512), _tile(n, 256)
    xs = _dispatch(h2, pos.reshape(n // td, 1, 2 * td), zero_tiles, n_tiles, td)
    y = _experts(xs, tile_expert, n_used, wg, wu, wd, n_tiles)
    return _combine(x_mid, route, y, pos.reshape(n // tc, 1, 2 * tc), final_norm, final, tc)


def _tile(n, pref):
    t = min(n, pref)
    assert n % t == 0, (n, t)
    return t


def kernel(x, positions, attn_norm, w_in, b_gate, conv_w, w_out_conv, q_norm, kv_norm, w_uq, w_ukv, w_out_mla, w_out_sb, w_o, ffn_norm, w_router_group, b_router_group, w_router_expert, b_router_expert, w_exp_gate, w_exp_up, w_exp_down, final_norm):
    batch, seq, d = x.shape
    assert d == D_MODEL
    n = batch * seq
    depth = w_in.shape[0]
    tm_rows = _tile(seq, 512)
    tq = _tile(seq, 256)
    x2 = x.reshape(n, d)
    cos, sin = _rope_tables(positions, _tile(n, 1024))
    for l in range(depth):
        gates, conv, lat, sb = _in_proj(x2, attn_norm[l][None, :], _pack_w_in(w_in[l]), b_gate[l][None, :], tm_rows)
        wq_main, wq_swap, wk, wv = _pack_mla_weights(w_uq[l], w_ukv[l])
        q, k, v = _mla_prep(lat, cos, sin, q_norm[l][None, :], kv_norm[l][None, :], wq_main, wq_swap, wk, wv,
                            tm_rows)
        o_mla = _mla_attn(q, k, v, batch, seq, tq)
        o_sb = _sb_attn(sb, batch, seq, tq)
        wr_hi, wr_lo, b_router = _pack_router(w_router_group[l], b_router_group[l], w_router_expert[l],
                                              b_router_expert[l])
        conv_w8 = jnp.concatenate([conv_w[l], jnp.zeros((8 - CONV_K, CONV_WIDTH), F32)], axis=0)
        x_mid, h2, route, counts = _merge(x2, gates, conv, o_mla, o_sb, conv_w8, w_out_conv[l].astype(BF16),
                                          w_out_mla[l].astype(BF16), w_out_sb[l].astype(BF16),
                                          w_o[l].astype(BF16), ffn_norm[l][None, :], wr_hi, wr_lo, b_router, seq,
                                          tm_rows)
        x2 = _moe(x_mid, h2, route, counts, w_exp_gate[l].astype(BF16), w_exp_up[l].astype(BF16),
                  w_exp_down[l].astype(BF16), final_norm[None, :], l == depth - 1)
    return x2.reshape(batch, seq, d)
```

```python
import functools

import jax
import jax.numpy as jnp
from jax import lax
from jax.experimental import pallas as pl
from jax.experimental.pallas import tpu as pltpu

D_MODEL = 1024
CONV_WIDTH = 512
CONV_K = 3
MLA_HEADS = 8
MLA_NOPE = 64
MLA_ROPE = 32
MLA_V = 64
MLA_Q_RANK = 256
MLA_KV_RANK = 128
ROPE_THETA = 10000.0
SB_HEADS = 8
SB_HEAD_DIM = 64
SB_WIDTH = SB_HEADS * SB_HEAD_DIM
N_BRANCHES = 3
OFF_CONV = 0
OFF_CQ = OFF_CONV + 3 * CONV_WIDTH
OFF_CKV = OFF_CQ + MLA_Q_RANK
OFF_KR = OFF_CKV + MLA_KV_RANK
OFF_SB = OFF_KR + MLA_ROPE
OFF_GATE = OFF_SB + 3 * SB_WIDTH
N_GROUPS = 4
EXPERTS_PER_GROUP = 8
N_EXPERTS = N_GROUPS * EXPERTS_PER_GROUP
D_EXPERT = 256
EPS = 1e-6
LOG2E = 1.4426950408889634
SB_DEAD = 160.0

LANES = 128
HEAD_PAD = 128
ROPE_HALF = MLA_ROPE // 2
GATE_COLS = N_BRANCHES * D_MODEL
CONV_COLS = 3 * CONV_WIDTH
LAT_COLS = MLA_Q_RANK + MLA_KV_RANK + 2 * HEAD_PAD
SB_COLS = 3 * SB_WIDTH
ROUTER_COLS = LANES
ROW_TILES = D_MODEL // LANES
MOE_TILE = 256
VMEM_LIMIT = 56 * 1024 * 1024

BF16 = jnp.bfloat16
F32 = jnp.float32


def _params(n_axes, vmem=VMEM_LIMIT):
    return pltpu.CompilerParams(dimension_semantics=("arbitrary",) * n_axes, vmem_limit_bytes=vmem)


def _rms(xf, gain):
    return xf * lax.rsqrt(jnp.mean(xf * xf, axis=-1, keepdims=True) + EPS) * gain


def _const_spec(shape):
    return pl.BlockSpec(shape, lambda *_: (0,) * len(shape))


def _rope_table_kernel(pos_ref, freq_ref, cos_ref, sin_ref):
    ang = pos_ref[...].astype(F32) * freq_ref[...]
    cos_ref[...] = jnp.cos(ang)
    sin_ref[...] = jnp.sin(ang)


def _rope_tables(positions, tm):
    n = positions.size
    half = ROPE_HALF
    freqs = ROPE_THETA ** (-jnp.arange(half, dtype=F32) / half)
    zeros = jnp.zeros((MLA_NOPE,), F32)
    freq_row = jnp.concatenate([zeros, freqs, freqs, jnp.zeros((HEAD_PAD - MLA_NOPE - MLA_ROPE,), F32)])[None, :]
    pos = positions.reshape(n, 1)
    return pl.pallas_call(
        _rope_table_kernel,
        grid=(n // tm,),
        in_specs=[pl.BlockSpec((tm, 1), lambda i: (i, 0)), _const_spec((1, HEAD_PAD))],
        out_specs=[pl.BlockSpec((tm, HEAD_PAD), lambda i: (i, 0))] * 2,
        out_shape=[jax.ShapeDtypeStruct((n, HEAD_PAD), F32)] * 2,
        compiler_params=_params(1),
        name="rope_tables",
    )(pos, freq_row)


def _token_rows(ref, first_token, count=1):
    start = pl.multiple_of(first_token * ROW_TILES, ROW_TILES)
    return ref.at[pl.ds(start, count * ROW_TILES)]


def _combine_steps(pos_ref, posn_ref, x_ref, route_ref, y_hbm, ybuf, sem, *, tc, n_steps, consume,
                   spread_issue=False):
    t = pl.program_id(0)

    def issue(p_ref, slot, part=0, parts=1):
        for r in range(2 * tc * part // parts, 2 * tc * (part + 1) // parts):
            pltpu.make_async_copy(_token_rows(y_hbm, p_ref[0, 0, r]),
                                  ybuf.at[slot, pl.ds(r * ROW_TILES, ROW_TILES)], sem.at[slot]).start()

    def wait(slot):
        pltpu.make_async_copy(_token_rows(y_hbm, 0, 2 * tc), ybuf.at[slot], sem.at[slot]).wait()

    @pl.when(t == 0)
    def _():
        issue(pos_ref, 0)

    def run(slot):
        if not spread_issue:
            issue(posn_ref, 1 - slot)
        wait(slot)
        w1, w2 = route_ref[:, 0:1], route_ref[:, 1:2]
        outs = []
        for c in range(ROW_TILES):
            y1 = ybuf[slot, pl.ds(c, tc, stride=2 * ROW_TILES), :]
            y2 = ybuf[slot, pl.ds(ROW_TILES + c, tc, stride=2 * ROW_TILES), :]
            outs.append(x_ref[:, c * LANES:(c + 1) * LANES] + w1 * y1 + w2 * y2)
        issue_part = functools.partial(issue, posn_ref, 1 - slot) if spread_issue else None
        consume(jnp.concatenate(outs, axis=1), issue_part)

    for slot in range(2):
        @pl.when(t % 2 == slot)
        def _():
            run(slot)

    @pl.when(t == n_steps - 1)
    def _():
        wait(n_steps % 2)


def _combine_specs(tc, n_steps):
    pos_spec = lambda step: pl.BlockSpec((1, 1, 2 * tc), lambda t: (step(t), 0, 0), memory_space=pltpu.SMEM)
    row = lambda w: pl.BlockSpec((tc, w), lambda t: (t, 0))
    in_specs = [pos_spec(lambda t: t), pos_spec(lambda t: jnp.minimum(t + 1, n_steps - 1)), row(D_MODEL),
                row(ROUTER_COLS), pl.BlockSpec(memory_space=pl.ANY)]
    scratch = [pltpu.VMEM((2, 2 * tc * ROW_TILES, LANES), F32), pltpu.SemaphoreType.DMA((2,))]
    return in_specs, scratch


def _project(x, g_ref, w_ref, b_ref, gate_ref, conv_ref, lat_ref, sb_ref, chunk, before_chunk=None):
    h = _rms(x, g_ref[...]).astype(BF16)
    n_chunks = sum(-(-w // chunk) for w in PROJ_WIDTHS)
    done = [0]

    def run(out_ref, col0, width, epilogue):
        for c in range(0, width, chunk):
            cw = min(chunk, width - c)
            if before_chunk is not None:
                before_chunk(done[0], n_chunks)
            done[0] += 1
            acc = jnp.dot(h, w_ref[:, col0 + c:col0 + c + cw], preferred_element_type=F32)
            out_ref[:, c:c + cw] = epilogue(acc, c, cw).astype(out_ref.dtype)

    run(gate_ref, 0, GATE_COLS, lambda a, c, cw: jax.nn.sigmoid(a + b_ref[:, c:c + cw]))
    ident = lambda a, c, cw: a
    run(conv_ref, GATE_COLS, CONV_COLS, ident)
    run(lat_ref, GATE_COLS + CONV_COLS, LAT_COLS, ident)
    run(sb_ref, GATE_COLS + CONV_COLS + LAT_COLS, SB_COLS, ident)


PROJ_WIDTHS = (GATE_COLS, CONV_COLS, LAT_COLS, SB_COLS)
PROJ_CHUNK = 512


def _in_proj_kernel(x_ref, g_ref, w_ref, b_ref, gate_ref, conv_ref, lat_ref, sb_ref):
    _project(x_ref[...], g_ref, w_ref, b_ref, gate_ref, conv_ref, lat_ref, sb_ref, PROJ_CHUNK)


def _in_proj(x2, gain, w_packed, b_gate, tm):
    n = x2.shape[0]
    return pl.pallas_call(
        _in_proj_kernel,
        grid=(n // tm,),
        in_specs=[pl.BlockSpec((tm, D_MODEL), lambda i: (i, 0)), _const_spec((1, D_MODEL)),
                  _const_spec(w_packed.shape), _const_spec((1, GATE_COLS))],
        out_specs=[pl.BlockSpec((tm, w), lambda i: (i, 0)) for w in PROJ_WIDTHS],
        out_shape=[jax.ShapeDtypeStruct((n, w), BF16) for w in PROJ_WIDTHS],
        compiler_params=_params(1),
        name="in_proj",
    )(x2, gain, w_packed, b_gate)


def _combine_in_proj_kernel(pos_ref, posn_ref, x_ref, route_ref, y_hbm, g_ref, w_ref, b_ref,
                            xo_ref, gate_ref, conv_ref, lat_ref, sb_ref, ybuf, sem, *, tm, n_steps):
    def consume(x, issue_part):
        xo_ref[...] = x
        _project(x, g_ref, w_ref, b_ref, gate_ref, conv_ref, lat_ref, sb_ref, PROJ_CHUNK, before_chunk=issue_part)

    _combine_steps(pos_ref, posn_ref, x_ref, route_ref, y_hbm, ybuf, sem, tc=tm, n_steps=n_steps, consume=consume,
                   spread_issue=True)


def _combine_in_proj(x_mid, route, y, pos, gain, w_packed, b_gate, tm):
    n = x_mid.shape[0]
    n_steps = n // tm
    in_specs, scratch = _combine_specs(tm, n_steps)
    pos3 = pos.reshape(n_steps, 1, 2 * tm)
    widths = (D_MODEL,) + PROJ_WIDTHS
    return pl.pallas_call(
        functools.partial(_combine_in_proj_kernel, tm=tm, n_steps=n_steps),
        grid=(n_steps,),
        in_specs=in_specs + [_const_spec((1, D_MODEL)),
                             pl.BlockSpec(w_packed.shape, lambda t: (0, 0), pipeline_mode=pl.Buffered(1)),
                             _const_spec((1, GATE_COLS))],
        out_specs=[pl.BlockSpec((tm, w), lambda t: (t, 0)) for w in widths],
        out_shape=[jax.ShapeDtypeStruct((n, D_MODEL), F32)] + [jax.ShapeDtypeStruct((n, w), BF16)
                                                               for w in PROJ_WIDTHS],
        scratch_shapes=scratch,
        compiler_params=_params(1),
        name="combine_in_proj",
    )(pos3, pos3, x_mid, route, y, gain, w_packed, b_gate)


def _pack_w_in(w):
    kr = w[:, OFF_KR:OFF_SB]
    x1, x2 = kr[:, :ROPE_HALF], kr[:, ROPE_HALF:]
    z_lo = jnp.zeros((D_MODEL, MLA_NOPE), w.dtype)
    z_hi = jnp.zeros((D_MODEL, HEAD_PAD - MLA_NOPE - MLA_ROPE), w.dtype)
    kr_main = jnp.concatenate([z_lo, x1, x2, z_hi], axis=1)
    kr_swap = jnp.concatenate([z_lo, -x2, x1, z_hi], axis=1)
    q_sb = w[:, OFF_SB:OFF_SB + SB_WIDTH] * (LOG2E * SB_HEAD_DIM ** -0.5)
    return jnp.concatenate([w[:, OFF_GATE:], w[:, OFF_CONV:OFF_CQ], w[:, OFF_CQ:OFF_KR], kr_main, kr_swap,
                            q_sb, w[:, OFF_SB + SB_WIDTH:OFF_GATE]], axis=1).astype(BF16)


def _mla_prep_kernel(lat_ref, cos_ref, sin_ref, qn_ref, kvn_ref, wqm_ref, wqs_ref, wk_ref, wv_ref,
                     q_ref, k_ref, v_ref, *, scale):
    cos, sin = cos_ref[...], sin_ref[...]
    cq = _rms(lat_ref[:, :MLA_Q_RANK].astype(F32), qn_ref[...]).astype(BF16)
    ckv = _rms(lat_ref[:, MLA_Q_RANK:MLA_Q_RANK + MLA_KV_RANK].astype(F32), kvn_ref[...]).astype(BF16)
    kr0 = MLA_Q_RANK + MLA_KV_RANK
    kr = (lat_ref[:, kr0:kr0 + HEAD_PAD].astype(F32) * cos
          + lat_ref[:, kr0 + HEAD_PAD:kr0 + 2 * HEAD_PAD].astype(F32) * sin)
    qm = jnp.dot(cq, wqm_ref[...], preferred_element_type=F32)
    qs = jnp.dot(cq, wqs_ref[...], preferred_element_type=F32)
    kn = jnp.dot(ckv, wk_ref[...], preferred_element_type=F32)
    for h in range(MLA_HEADS):
        sl = slice(h * HEAD_PAD, (h + 1) * HEAD_PAD)
        q_ref[:, sl] = ((qm[:, sl] * cos + qs[:, sl] * sin) * scale).astype(BF16)
        k_ref[:, sl] = (kn[:, sl] + kr).astype(BF16)
    v = jnp.dot(ckv, wv_ref[...], preferred_element_type=F32)
    lane = lax.broadcasted_iota(jnp.int32, v.shape, 1)
    v_ref[...] = jnp.where(lane % HEAD_PAD == MLA_V, 1.0, v).astype(BF16)


def _mla_prep(lat, cos, sin, q_norm, kv_norm, wq_main, wq_swap, wk, wv, tm):
    n = lat.shape[0]
    hw = MLA_HEADS * HEAD_PAD
    row = lambda w: pl.BlockSpec((tm, w), lambda i: (i, 0))
    return pl.pallas_call(
        functools.partial(_mla_prep_kernel, scale=LOG2E * (MLA_NOPE + MLA_ROPE) ** -0.5),
        grid=(n // tm,),
        in_specs=[row(LAT_COLS), row(HEAD_PAD), row(HEAD_PAD), _const_spec((1, MLA_Q_RANK)),
                  _const_spec((1, MLA_KV_RANK)), _const_spec((MLA_Q_RANK, hw)), _const_spec((MLA_Q_RANK, hw)),
                  _const_spec((MLA_KV_RANK, hw)), _const_spec((MLA_KV_RANK, hw))],
        out_specs=[row(hw), row(hw), row(hw)],
        out_shape=[jax.ShapeDtypeStruct((n, hw), BF16)] * 3,
        compiler_params=_params(1),
        name="mla_prep",
    )(lat, cos, sin, q_norm, kv_norm, wq_main, wq_swap, wk, wv)


def _pack_mla_weights(w_uq, w_ukv):
    qd = MLA_NOPE + MLA_ROPE
    z_hi = jnp.zeros((MLA_Q_RANK, HEAD_PAD - qd), w_uq.dtype)
    z_lo = jnp.zeros((MLA_Q_RANK, MLA_NOPE), w_uq.dtype)
    main, swap, wk, wv = [], [], [], []
    for h in range(MLA_HEADS):
        wq = w_uq[:, h * qd:(h + 1) * qd]
        nope, x1, x2 = wq[:, :MLA_NOPE], wq[:, MLA_NOPE:MLA_NOPE + ROPE_HALF], wq[:, MLA_NOPE + ROPE_HALF:]
        main += [nope, x1, x2, z_hi]
        swap += [z_lo, -x2, x1, z_hi]
        kv = w_ukv[:, h * (MLA_NOPE + MLA_V):(h + 1) * (MLA_NOPE + MLA_V)]
        wk += [kv[:, :MLA_NOPE], jnp.zeros((MLA_KV_RANK, HEAD_PAD - MLA_NOPE), w_ukv.dtype)]
        wv += [kv[:, MLA_NOPE:], jnp.zeros((MLA_KV_RANK, HEAD_PAD - MLA_V), w_ukv.dtype)]
    cat = lambda parts: jnp.concatenate(parts, axis=1).astype(BF16)
    return cat(main), cat(swap), cat(wk), cat(wv)


def _mla_attn_kernel(q_ref, k_ref, v_ref, o_ref, m_ref, acc_ref, *, tq):
    i = pl.program_id(1)
    row = lax.broadcasted_iota(jnp.int32, (tq, tq), 0)
    col = lax.broadcasted_iota(jnp.int32, (tq, tq), 1)
    causal = row >= col
    dn = (((1,), (1,)), ((), ()))

    def step(j, width, diagonal=False):
        start = pl.multiple_of(j * tq, tq)
        for h in range(MLA_HEADS):
            hs = slice(h * HEAD_PAD, (h + 1) * HEAD_PAD)
            s = lax.dot_general(q_ref[:, hs], k_ref[pl.ds(start, width), hs], dn, preferred_element_type=F32)
            if diagonal:
                s = jnp.where(causal, s, -jnp.inf)
            m_cur = jnp.max(s, axis=1, keepdims=True)
            if diagonal:
                m_new = jnp.broadcast_to(m_cur, (tq, LANES))
            else:
                m_old = m_ref[h]
                m_new = jnp.maximum(m_old, m_cur)
            p = jnp.exp2(s - jnp.concatenate([m_new] * (width // LANES), axis=1))
            pv = jnp.dot(p.astype(BF16), v_ref[pl.ds(start, width), hs], preferred_element_type=F32)
            if diagonal:
                acc_ref[h] = pv
            else:
                acc_ref[h] = jnp.exp2(m_old - m_new) * acc_ref[h] + pv
            m_ref[h] = m_new

    step(i, tq, diagonal=True)

    unroll = 4

    def body(jj, carry):
        for u in range(unroll):
            step(unroll * jj + u, tq)
        return carry

    lax.fori_loop(0, i // unroll, body, 0)

    rest = i - i % unroll

    @pl.when(i % unroll >= 2)
    def _():
        step(rest, tq)
        step(rest + 1, tq)

    @pl.when(i % 2 == 1)
    def _():
        step(i - 1, tq)

    outs = []
    for h in range(MLA_HEADS):
        acc = acc_ref[h]
        outs.append(acc[:, :MLA_V] / acc[:, MLA_V:MLA_V + 1])
    o_ref[...] = jnp.concatenate(outs, axis=1).astype(o_ref.dtype)


def _mla_attn(q, k, v, batch, seq, tq):
    n = q.shape[0]
    hw = MLA_HEADS * HEAD_PAD
    vw = MLA_HEADS * MLA_V
    nq = seq // tq
    return pl.pallas_call(
        functools.partial(_mla_attn_kernel, tq=tq),
        grid=(batch, nq),
        in_specs=[pl.BlockSpec((tq, hw), lambda b, i: (b * nq + i, 0)),
                  pl.BlockSpec((seq, hw), lambda b, i: (b, 0)),
                  pl.BlockSpec((seq, hw), lambda b, i: (b, 0))],
        out_specs=pl.BlockSpec((tq, vw), lambda b, i: (b * nq + i, 0)),
        out_shape=jax.ShapeDtypeStruct((n, vw), BF16),
        scratch_shapes=[pltpu.VMEM((MLA_HEADS, tq, LANES), F32), pltpu.VMEM((MLA_HEADS, tq, HEAD_PAD), F32)],
        compiler_params=_params(2),
        name="mla_attn",
    )(q, k, v)


def _sb_attn_kernel(q_ref, k_ref, v_ref, tri_ref, o_ref, run_ref, acc_ref, *, tq):
    i = pl.program_id(1)
    row = lax.broadcasted_iota(jnp.int32, (tq, tq), 0)
    col = lax.broadcasted_iota(jnp.int32, (tq, tq), 1)
    strict = row > col
    dn = (((1,), (1,)), ((), ()))
    tri = tri_ref[...]

    def step(j, diagonal):
        start = pl.multiple_of(j * tq, tq)
        for h in range(SB_HEADS):
            hs = slice(h * SB_HEAD_DIM, (h + 1) * SB_HEAD_DIM)
            z = lax.dot_general(q_ref[:, hs], k_ref[pl.ds(start, tq), hs], dn, preferred_element_type=F32)
            sp = jnp.maximum(z, 0.0) + jnp.log(1.0 + jnp.exp2(-jnp.abs(z))) * LOG2E
            if diagonal:
                sp = jnp.where(strict, sp, 0.0)
            later = jnp.dot(sp.astype(BF16), tri, preferred_element_type=F32)
            a = jnp.exp2(z - sp - later)
            if diagonal:
                a = jnp.where(strict, a, 0.0)
            av = jnp.dot(a.astype(BF16), v_ref[pl.ds(start, tq), hs], preferred_element_type=F32)
            total = jnp.sum(sp, axis=1, keepdims=True)
            if diagonal:
                acc_ref[h] = av
                run_ref[h] = jnp.broadcast_to(total, (tq, LANES))
            else:
                run = run_ref[h]
                acc_ref[h] += jnp.exp2(-run[:, :SB_HEAD_DIM]) * av
                run_ref[h] = run + total

    @pl.when(i == 0)
    def _():
        step(0, True)

    @pl.when(i > 0)
    def _():
        step(i, True)
        step(i - 1, False)

    def alive():
        return (jnp.min(run_ref[...]) < SB_DEAD).astype(jnp.int32)

    def cond(carry):
        t, live = carry
        return (t < i) & (live > 0)

    def body(carry):
        t, _ = carry
        step(i - 1 - t, False)
        return t + 1, alive()

    lax.while_loop(cond, body, (jnp.int32(1), alive()))
    o_ref[...] = jnp.concatenate([acc_ref[h] for h in range(SB_HEADS)], axis=1).astype(o_ref.dtype)


def _sb_attn(qkv, batch, seq, tq):
    n = qkv.shape[0]
    nq = seq // tq
    j = jnp.arange(tq)
    tri = (j[:, None] > j[None, :]).astype(BF16)
    return pl.pallas_call(
        functools.partial(_sb_attn_kernel, tq=tq),
        grid=(batch, nq),
        in_specs=[pl.BlockSpec((tq, SB_WIDTH), lambda b, i: (b * nq + i, 0)),
                  pl.BlockSpec((seq, SB_WIDTH), lambda b, i: (b, 1)),
                  pl.BlockSpec((seq, SB_WIDTH), lambda b, i: (b, 2)),
                  _const_spec((tq, tq))],
        out_specs=pl.BlockSpec((tq, SB_WIDTH), lambda b, i: (b * nq + i, 0)),
        out_shape=jax.ShapeDtypeStruct((n, SB_WIDTH), BF16),
        scratch_shapes=[pltpu.VMEM((SB_HEADS, tq, LANES), F32), pltpu.VMEM((SB_HEADS, tq, SB_HEAD_DIM), F32)],
        compiler_params=_params(2),
        name="sb_attn",
    )(qkv, qkv, qkv, tri)


def _merge_kernel(x_ref, gate_ref, conv_ref, halo_ref, omla_ref, osb_ref, cw_ref, wa_ref, wb_ref, wc_ref,
                  wo_ref, fn_ref, wr_hi_ref, wr_lo_ref, br_ref, low_ref, xo_ref, h2_ref, route_ref, cnt_ref,
                  *, tm, tiles_per_seq):
    i = pl.program_id(0)
    f = lambda r: r.astype(F32)
    conv = conv_ref[...]
    u = f(conv[:, 2 * CONV_WIDTH:]) * f(conv[:, :CONV_WIDTH])
    halo = halo_ref[...]
    first = (i % tiles_per_seq) == 0
    up = f(halo[:, 2 * CONV_WIDTH:]) * f(halo[:, :CONV_WIDTH])
    up = jnp.where(first, 0.0, up)
    ue = jnp.concatenate([up, u], axis=0)
    cw = cw_ref[...]
    y = cw[0:1, :] * ue[6:tm + 6] + cw[1:2, :] * ue[7:tm + 7] + cw[2:3, :] * u
    ya = (f(conv[:, CONV_WIDTH:2 * CONV_WIDTH]) * y).astype(BF16)
    gate = gate_ref[...]
    merged = (f(gate[:, :D_MODEL]) * jnp.dot(ya, wa_ref[...], preferred_element_type=F32)
              + f(gate[:, D_MODEL:2 * D_MODEL]) * jnp.dot(omla_ref[...], wb_ref[...], preferred_element_type=F32)
              + f(gate[:, 2 * D_MODEL:]) * jnp.dot(osb_ref[...], wc_ref[...], preferred_element_type=F32))
    x_new = x_ref[...] + jnp.dot(merged.astype(BF16), wo_ref[...], preferred_element_type=F32)
    xo_ref[...] = x_new
    h2 = _rms(x_new, fn_ref[...])
    h2_hi = h2.astype(BF16)
    for c in range(ROW_TILES):
        h2_ref[pl.ds(c, tm, stride=ROW_TILES), :] = h2_hi[:, c * LANES:(c + 1) * LANES].astype(F32)
    h2_lo = (h2 - h2_hi.astype(F32)).astype(BF16)
    logits = (jnp.dot(h2_hi, wr_hi_ref[...], preferred_element_type=F32)
              + jnp.dot(h2_lo, wr_hi_ref[...], preferred_element_type=F32)
              + jnp.dot(h2_hi, wr_lo_ref[...], preferred_element_type=F32)) + br_ref[...]
    w1, w2, i1, i2 = _route(logits)

    @pl.when(i == 0)
    def _():
        cnt_ref[...] = jnp.zeros_like(cnt_ref)

    lane = lax.broadcasted_iota(jnp.int32, logits.shape, 1)
    pick1, pick2 = lane == i1, lane == i2
    onehot = jnp.where(pick1 | pick2, 1.0, 0.0)
    before = jnp.dot(low_ref[...], onehot.astype(BF16), preferred_element_type=F32) + cnt_ref[...]
    rank1 = jnp.sum(jnp.where(pick1, before, 0.0), axis=-1, keepdims=True)
    rank2 = jnp.sum(jnp.where(pick2, before, 0.0), axis=-1, keepdims=True)
    cnt_ref[...] += jnp.sum(onehot, axis=0, keepdims=True)
    cols = (w1, w2, i1.astype(F32), i2.astype(F32), rank1, rank2)
    route = jnp.zeros(logits.shape, F32)
    for k, col in enumerate(cols):
        route = jnp.where(lane == k, col, route)
    route_ref[...] = route


def _route(logits):
    lane = lax.broadcasted_iota(jnp.int32, logits.shape, 1)
    big = jnp.int32(1 << 30)
    neg = -jnp.inf
    is_group = (lane >= N_EXPERTS) & (lane < N_EXPERTS + N_GROUPS)
    gl = jnp.where(is_group, logits, neg)
    gmax = jnp.max(gl, axis=-1, keepdims=True)
    g_w = 1.0 / jnp.sum(jnp.exp(gl - gmax), axis=-1, keepdims=True)
    g_idx = jnp.min(jnp.where(gl == gmax, lane, big), axis=-1, keepdims=True) - N_EXPERTS
    lo = g_idx * EXPERTS_PER_GROUP
    el = jnp.where((lane >= lo) & (lane < lo + EXPERTS_PER_GROUP), logits, neg)
    m1 = jnp.max(el, axis=-1, keepdims=True)
    i1 = jnp.min(jnp.where(el == m1, lane, big), axis=-1, keepdims=True)
    el2 = jnp.where(lane == i1, neg, el)
    m2 = jnp.max(el2, axis=-1, keepdims=True)
    i2 = jnp.min(jnp.where(el2 == m2, lane, big), axis=-1, keepdims=True)
    r = jnp.exp(m2 - m1)
    w1 = g_w / (1.0 + r)
    w2 = g_w * r / (1.0 + r)
    return w1, w2, i1, i2


def _merge(x2, gates, conv, o_mla, o_sb, conv_w, wa, wb, wc, wo, ffn_norm, wr_hi, wr_lo, b_router, seq, tm):
    n = x2.shape[0]
    row = lambda w: pl.BlockSpec((tm, w), lambda i: (i, 0))
    sub = tm // 8
    halo = pl.BlockSpec((8, CONV_COLS), lambda i: (jnp.maximum(i * sub - 1, 0), 0))
    r = jnp.arange(tm)
    lower = (r[:, None] > r[None, :]).astype(BF16)
    return pl.pallas_call(
        functools.partial(_merge_kernel, tm=tm, tiles_per_seq=seq // tm),
        grid=(n // tm,),
        in_specs=[row(D_MODEL), row(GATE_COLS), row(CONV_COLS), halo, row(CONV_WIDTH), row(SB_WIDTH),
                  _const_spec((8, CONV_WIDTH)), _const_spec((CONV_WIDTH, D_MODEL)),
                  _const_spec((MLA_HEADS * MLA_V, D_MODEL)), _const_spec((SB_WIDTH, D_MODEL)),
                  _const_spec((D_MODEL, D_MODEL)), _const_spec((1, D_MODEL)),
                  _const_spec((D_MODEL, ROUTER_COLS)), _const_spec((D_MODEL, ROUTER_COLS)),
                  _const_spec((1, ROUTER_COLS)), _const_spec((tm, tm))],
        out_specs=[row(D_MODEL), pl.BlockSpec((tm * ROW_TILES, LANES), lambda i: (i, 0)), row(ROUTER_COLS),
                   _const_spec((1, ROUTER_COLS))],
        out_shape=[jax.ShapeDtypeStruct((n, D_MODEL), F32), jax.ShapeDtypeStruct((n * ROW_TILES, LANES), F32),
                   jax.ShapeDtypeStruct((n, ROUTER_COLS), F32), jax.ShapeDtypeStruct((1, ROUTER_COLS), F32)],
        compiler_params=_params(1),
        name="merge",
    )(x2, gates, conv, conv, o_mla, o_sb, conv_w, wa, wb, wc, wo, ffn_norm, wr_hi, wr_lo, b_router, lower)


def _pack_router(w_rg, b_rg, w_re, b_re):
    pad = ROUTER_COLS - N_EXPERTS - N_GROUPS
    w = jnp.concatenate([w_re, w_rg, jnp.zeros((D_MODEL, pad), F32)], axis=1)
    b = jnp.concatenate([b_re, b_rg, jnp.zeros((pad,), F32)])[None, :]
    hi = w.astype(BF16)
    lo = (w - hi.astype(F32)).astype(BF16)
    return hi, lo, b


def _moe_plan(route, counts, n):
    cnt = counts[0, :N_EXPERTS].astype(jnp.int32)
    padded = (cnt + MOE_TILE - 1) // MOE_TILE * MOE_TILE
    ends = jnp.cumsum(padded)
    starts = ends - padded
    n_used = ends[-1] // MOE_TILE
    expert = route[:, 2:4].astype(jnp.int32)
    rank = route[:, 4:6].astype(jnp.int32)
    chosen = expert[..., None] == jnp.arange(N_EXPERTS, dtype=jnp.int32)
    pos = jnp.sum(jnp.where(chosen, starts, 0), axis=-1) + rank
    n_tiles = (2 * n) // MOE_TILE + N_EXPERTS
    tile_start = jnp.minimum(jnp.arange(n_tiles, dtype=jnp.int32), n_used - 1) * MOE_TILE
    tile_expert = jnp.sum((ends[None, :] <= tile_start[:, None]).astype(jnp.int32), axis=1)
    last_tile = jnp.where(padded > 0, ends // MOE_TILE - 1, -1)
    tail = n_used + jnp.arange(N_EXPERTS, dtype=jnp.int32)
    zero_tiles = jnp.concatenate([last_tile, jnp.where(tail < n_tiles, tail, -1)])
    return pos, tile_expert, n_used.reshape(1), zero_tiles, n_tiles


def _dispatch_kernel(zt_ref, pos_ref, h_ref, xs_hbm, zero_ref, sem, zsem, *, td):
    t = pl.program_id(0)

    def fill(j):
        return pltpu.make_async_copy(zero_ref, _token_rows(xs_hbm, zt_ref[j] * MOE_TILE, MOE_TILE), zsem)

    @pl.when(t == 0)
    def _():
        zero_ref[...] = jnp.zeros_like(zero_ref)
        for j in range(2 * N_EXPERTS):
            @pl.when(zt_ref[j] >= 0)
            def _():
                fill(j).start()
        for j in range(2 * N_EXPERTS):
            @pl.when(zt_ref[j] >= 0)
            def _():
                fill(j).wait()

    for r in range(td):
        src = h_ref.at[pl.ds(r * ROW_TILES, ROW_TILES)]
        for k in range(2):
            pltpu.make_async_copy(src, _token_rows(xs_hbm, pos_ref[0, 0, 2 * r + k]), sem).start()

    for k in range(2):
        pltpu.make_async_copy(h_ref, _token_rows(xs_hbm, 0, td), sem).wait()


def _dispatch(h2, pos3, zero_tiles, n_tiles, td):
    n = h2.shape[0] // ROW_TILES
    n_steps = n // td
    return pl.pallas_call(
        functools.partial(_dispatch_kernel, td=td),
        grid_spec=pltpu.PrefetchScalarGridSpec(
            num_scalar_prefetch=1, grid=(n_steps,),
            in_specs=[pl.BlockSpec((1, 1, 2 * td), lambda t, zt: (t, 0, 0), memory_space=pltpu.SMEM),
                      pl.BlockSpec((td * ROW_TILES, LANES), lambda t, zt: (t, 0))],
            out_specs=pl.BlockSpec(memory_space=pl.ANY),
            scratch_shapes=[pltpu.VMEM((MOE_TILE * ROW_TILES, LANES), F32), pltpu.SemaphoreType.DMA,
                            pltpu.SemaphoreType.DMA]),
        out_shape=jax.ShapeDtypeStruct((n_tiles * MOE_TILE * ROW_TILES, LANES), F32),
        compiler_params=_params(1),
        name="moe_dispatch",
    )(zero_tiles, pos3, h2)


def _experts_kernel(te_ref, nu_ref, xs_ref, wg_ref, wu_ref, wd_ref, y_ref, wg_bf, wu_bf, wd_bf):
    t = pl.program_id(0)

    @pl.when(t >= nu_ref[0])
    def _():
        y_ref[...] = jnp.zeros_like(y_ref)

    @pl.when((t == 0) | (te_ref[t] != te_ref[jnp.maximum(t - 1, 0)]))
    def _():
        wg_bf[...] = wg_ref[0].astype(BF16)
        wu_bf[...] = wu_ref[0].astype(BF16)
        wd_bf[...] = wd_ref[0].astype(BF16)

    @pl.when(t < nu_ref[0])
    def _():
        x = jnp.concatenate([xs_ref[pl.ds(c, MOE_TILE, stride=ROW_TILES), :] for c in range(ROW_TILES)],
                            axis=1).astype(BF16)
        g = jnp.dot(x, wg_bf[...], preferred_element_type=F32)
        u = jnp.dot(x, wu_bf[...], preferred_element_type=F32)
        act = (g * jax.nn.sigmoid(g) * u).astype(BF16)
        y = jnp.dot(act, wd_bf[...], preferred_element_type=F32)
        for c in range(ROW_TILES):
            y_ref[pl.ds(c, MOE_TILE, stride=ROW_TILES), :] = y[:, c * LANES:(c + 1) * LANES]


def _experts(xs, tile_expert, n_used, wg, wu, wd, layer, n_tiles):
    shape = (MOE_TILE * ROW_TILES, LANES)
    rows_in = pl.BlockSpec(shape, lambda t, te, nu: (jnp.minimum(t, nu[0] - 1), 0))
    rows_out = pl.BlockSpec(shape, lambda t, te, nu: (t, 0))
    weight = lambda shape: pl.BlockSpec((1,) + shape, lambda t, te, nu: (layer * N_EXPERTS + te[t], 0, 0))
    up, down = (D_MODEL, D_EXPERT), (D_EXPERT, D_MODEL)
    return pl.pallas_call(
        _experts_kernel,
        grid_spec=pltpu.PrefetchScalarGridSpec(
            num_scalar_prefetch=2, grid=(n_tiles,),
            in_specs=[rows_in, weight(up), weight(up), weight(down)],
            out_specs=rows_out,
            scratch_shapes=[pltpu.VMEM(up, BF16), pltpu.VMEM(up, BF16), pltpu.VMEM(down, BF16)]),
        out_shape=jax.ShapeDtypeStruct(xs.shape, F32),
        compiler_params=_params(1),
        name="moe_experts",
    )(tile_expert, n_used, xs, wg, wu, wd)


def _combine_norm_kernel(pos_ref, posn_ref, x_ref, route_ref, y_hbm, fin_ref, o_ref, ybuf, sem, *, tc, n_steps):
    def consume(x, issue_part):
        o_ref[...] = _rms(x, fin_ref[...])

    _combine_steps(pos_ref, posn_ref, x_ref, route_ref, y_hbm, ybuf, sem, tc=tc, n_steps=n_steps, consume=consume)


def _combine_norm(x_mid, route, y, pos, final_norm, tc):
    n = x_mid.shape[0]
    n_steps = n // tc
    in_specs, scratch = _combine_specs(tc, n_steps)
    pos3 = pos.reshape(n_steps, 1, 2 * tc)
    return pl.pallas_call(
        functools.partial(_combine_norm_kernel, tc=tc, n_steps=n_steps),
        grid=(n_steps,),
        in_specs=in_specs + [_const_spec((1, D_MODEL))],
        out_specs=pl.BlockSpec((tc, D_MODEL), lambda t: (t, 0)),
        out_shape=jax.ShapeDtypeStruct((n, D_MODEL), F32),
        scratch_shapes=scratch,
        compiler_params=_params(1),
        name="moe_combine_norm",
    )(pos3, pos3, x_mid, route, y, final_norm)


def _moe_experts(h2, route, counts, wg, wu, wd, layer):
    n = route.shape[0]
    pos, tile_expert, n_used, zero_tiles, n_tiles = _moe_plan(route, counts, n)
    td = _tile(n, 512)
    xs = _dispatch(h2, pos.reshape(n // td, 1, 2 * td), zero_tiles, n_tiles, td)
    return _experts(xs, tile_expert, n_used, wg, wu, wd, layer, n_tiles), pos


def _tile(n, pref):
    t = min(n, pref)
    assert n % t == 0, (n, t)
    return t


def kernel(x, positions, attn_norm, w_in, b_gate, conv_w, w_out_conv, q_norm, kv_norm, w_uq, w_ukv, w_out_mla, w_out_sb, w_o, ffn_norm, w_router_group, b_router_group, w_router_expert, b_router_expert, w_exp_gate, w_exp_up, w_exp_down, final_norm):
    batch, seq, d = x.shape
    assert d == D_MODEL
    n = batch * seq
    depth = w_in.shape[0]
    tm_rows = _tile(seq, 512)
    tq = _tile(seq, 256)
    x2 = x.reshape(n, d)
    cos, sin = _rope_tables(positions, _tile(n, 1024))
    wg_all = w_exp_gate.reshape(depth * N_EXPERTS, D_MODEL, D_EXPERT)
    wu_all = w_exp_up.reshape(depth * N_EXPERTS, D_MODEL, D_EXPERT)
    wd_all = w_exp_down.reshape(depth * N_EXPERTS, D_EXPERT, D_MODEL)
    moe = None
    for l in range(depth):
        proj_args = (attn_norm[l][None, :], _pack_w_in(w_in[l]), b_gate[l][None, :], tm_rows)
        if moe is None:
            gates, conv, lat, sb = _in_proj(x2, *proj_args)
        else:
            x2, gates, conv, lat, sb = _combine_in_proj(*moe, *proj_args)
        wq_main, wq_swap, wk, wv = _pack_mla_weights(w_uq[l], w_ukv[l])
        q, k, v = _mla_prep(lat, cos, sin, q_norm[l][None, :], kv_norm[l][None, :], wq_main, wq_swap, wk, wv,
                            tm_rows)
        o_mla = _mla_attn(q, k, v, batch, seq, tq)
        o_sb = _sb_attn(sb, batch, seq, tq)
        wr_hi, wr_lo, b_router = _pack_router(w_router_group[l], b_router_group[l], w_router_expert[l],
                                              b_router_expert[l])
        conv_w8 = jnp.concatenate([conv_w[l], jnp.zeros((8 - CONV_K, CONV_WIDTH), F32)], axis=0)
        x_mid, h2, route, counts = _merge(x2, gates, conv, o_mla, o_sb, conv_w8, w_out_conv[l].astype(BF16),
                                          w_out_mla[l].astype(BF16), w_out_sb[l].astype(BF16),
                                          w_o[l].astype(BF16), ffn_norm[l][None, :], wr_hi, wr_lo, b_router, seq,
                                          tm_rows)
        y, pos = _moe_experts(h2, route, counts, wg_all, wu_all, wd_all, l)
        moe = (x_mid, route, y, pos)
    out = _combine_norm(*moe, final_norm[None, :], _tile(n, 256))
    return out.reshape(batch, seq, d)
```

```python
import functools

import jax
import jax.numpy as jnp
from jax import lax
from jax.experimental import pallas as pl
from jax.experimental.pallas import tpu as pltpu

D_MODEL = 1024
CONV_WIDTH = 512
CONV_K = 3
MLA_HEADS = 8
MLA_NOPE = 64
MLA_ROPE = 32
MLA_V = 64
MLA_Q_RANK = 256
MLA_KV_RANK = 128
ROPE_THETA = 10000.0
SB_HEADS = 8
SB_HEAD_DIM = 64
SB_WIDTH = SB_HEADS * SB_HEAD_DIM
N_BRANCHES = 3
OFF_CONV = 0
OFF_CQ = OFF_CONV + 3 * CONV_WIDTH
OFF_CKV = OFF_CQ + MLA_Q_RANK
OFF_KR = OFF_CKV + MLA_KV_RANK
OFF_SB = OFF_KR + MLA_ROPE
OFF_GATE = OFF_SB + 3 * SB_WIDTH
N_GROUPS = 4
EXPERTS_PER_GROUP = 8
N_EXPERTS = N_GROUPS * EXPERTS_PER_GROUP
D_EXPERT = 256
EPS = 1e-6
LOG2E = 1.4426950408889634
SB_DEAD = 160.0

LANES = 128
HEAD_PAD = 128
ROPE_HALF = MLA_ROPE // 2
GATE_COLS = N_BRANCHES * D_MODEL
CONV_COLS = 3 * CONV_WIDTH
LAT_COLS = MLA_Q_RANK + MLA_KV_RANK + 2 * HEAD_PAD
SB_COLS = 3 * SB_WIDTH
ROUTER_COLS = LANES
ROW_TILES = D_MODEL // LANES
MOE_TILE = 256
VMEM_LIMIT = 56 * 1024 * 1024

BF16 = jnp.bfloat16
F32 = jnp.float32


def _params(n_axes, vmem=VMEM_LIMIT):
    return pltpu.CompilerParams(dimension_semantics=("arbitrary",) * n_axes, vmem_limit_bytes=vmem)


def _rms(xf, gain):
    return xf * lax.rsqrt(jnp.mean(xf * xf, axis=-1, keepdims=True) + EPS) * gain


def _const_spec(shape):
    return pl.BlockSpec(shape, lambda *_: (0,) * len(shape))


def _rope_table_kernel(pos_ref, freq_ref, cos_ref, sin_ref):
    ang = pos_ref[...].astype(F32) * freq_ref[...]
    cos_ref[...] = jnp.cos(ang)
    sin_ref[...] = jnp.sin(ang)


def _rope_tables(positions, tm):
    n = positions.size
    half = ROPE_HALF
    freqs = ROPE_THETA ** (-jnp.arange(half, dtype=F32) / half)
    zeros = jnp.zeros((MLA_NOPE,), F32)
    freq_row = jnp.concatenate([zeros, freqs, freqs, jnp.zeros((HEAD_PAD - MLA_NOPE - MLA_ROPE,), F32)])[None, :]
    pos = positions.reshape(n, 1)
    return pl.pallas_call(
        _rope_table_kernel,
        grid=(n // tm,),
        in_specs=[pl.BlockSpec((tm, 1), lambda i: (i, 0)), _const_spec((1, HEAD_PAD))],
        out_specs=[pl.BlockSpec((tm, HEAD_PAD), lambda i: (i, 0))] * 2,
        out_shape=[jax.ShapeDtypeStruct((n, HEAD_PAD), F32)] * 2,
        compiler_params=_params(1),
        name="rope_tables",
    )(pos, freq_row)


def _token_rows(ref, first_token, count=1):
    start = pl.multiple_of(first_token * ROW_TILES, ROW_TILES)
    return ref.at[pl.ds(start, count * ROW_TILES)]


def _combine_steps(pos_ref, posn_ref, x_ref, route_ref, y_hbm, ybuf, sem, *, tc, n_steps, consume,
                   spread_issue=False):
    t = pl.program_id(0)

    def issue(p_ref, slot, part=0, parts=1):
        for r in range(2 * tc * part // parts, 2 * tc * (part + 1) // parts):
            pltpu.make_async_copy(_token_rows(y_hbm, p_ref[0, 0, r]),
                                  ybuf.at[slot, pl.ds(r * ROW_TILES, ROW_TILES)], sem.at[slot]).start()

    def wait(slot):
        pltpu.make_async_copy(_token_rows(y_hbm, 0, 2 * tc), ybuf.at[slot], sem.at[slot]).wait()

    @pl.when(t == 0)
    def _():
        issue(pos_ref, 0)

    def run(slot):
        if not spread_issue:
            issue(posn_ref, 1 - slot)
        wait(slot)
        w1, w2 = route_ref[:, 0:1], route_ref[:, 1:2]
        outs = []
        for c in range(ROW_TILES):
            y1 = ybuf[slot, pl.ds(c, tc, stride=2 * ROW_TILES), :]
            y2 = ybuf[slot, pl.ds(ROW_TILES + c, tc, stride=2 * ROW_TILES), :]
            outs.append(x_ref[:, c * LANES:(c + 1) * LANES] + w1 * y1 + w2 * y2)
        issue_part = functools.partial(issue, posn_ref, 1 - slot) if spread_issue else None
        consume(jnp.concatenate(outs, axis=1), issue_part)

    for slot in range(2):
        @pl.when(t % 2 == slot)
        def _():
            run(slot)

    @pl.when(t == n_steps - 1)
    def _():
        wait(n_steps % 2)


def _combine_specs(tc, n_steps):
    pos_spec = lambda step: pl.BlockSpec((1, 1, 2 * tc), lambda t: (step(t), 0, 0), memory_space=pltpu.SMEM)
    row = lambda w: pl.BlockSpec((tc, w), lambda t: (t, 0))
    in_specs = [pos_spec(lambda t: t), pos_spec(lambda t: jnp.minimum(t + 1, n_steps - 1)), row(D_MODEL),
                row(ROUTER_COLS), pl.BlockSpec(memory_space=pl.ANY)]
    scratch = [pltpu.VMEM((2, 2 * tc * ROW_TILES, LANES), F32), pltpu.SemaphoreType.DMA((2,))]
    return in_specs, scratch


def _project(x, g_ref, w_ref, b_ref, gate_ref, conv_ref, lat_ref, sb_ref, chunk, before_chunk=None):
    h = _rms(x, g_ref[...]).astype(BF16)
    n_chunks = sum(-(-w // chunk) for w in PROJ_WIDTHS)
    done = [0]

    def run(out_ref, col0, width, epilogue):
        for c in range(0, width, chunk):
            cw = min(chunk, width - c)
            if before_chunk is not None:
                before_chunk(done[0], n_chunks)
            done[0] += 1
            acc = jnp.dot(h, w_ref[:, col0 + c:col0 + c + cw], preferred_element_type=F32)
            out_ref[:, c:c + cw] = epilogue(acc, c, cw).astype(out_ref.dtype)

    run(gate_ref, 0, GATE_COLS, lambda a, c, cw: jax.nn.sigmoid(a + b_ref[:, c:c + cw]))
    ident = lambda a, c, cw: a
    run(conv_ref, GATE_COLS, CONV_COLS, ident)
    run(lat_ref, GATE_COLS + CONV_COLS, LAT_COLS, ident)
    run(sb_ref, GATE_COLS + CONV_COLS + LAT_COLS, SB_COLS, ident)


PROJ_WIDTHS = (GATE_COLS, CONV_COLS, LAT_COLS, SB_COLS)
PROJ_CHUNK = 512


def _in_proj_kernel(x_ref, g_ref, w_ref, b_ref, gate_ref, conv_ref, lat_ref, sb_ref):
    _project(x_ref[...], g_ref, w_ref, b_ref, gate_ref, conv_ref, lat_ref, sb_ref, PROJ_CHUNK)


def _layer_spec(stacked, layer, **kwargs):
    shape = stacked.shape[1:]
    return pl.BlockSpec((None,) + shape, lambda *_: (layer,) + (0,) * len(shape), **kwargs)


def _in_proj(x2, gain, w_packed, layer, b_gate, tm):
    n = x2.shape[0]
    return pl.pallas_call(
        _in_proj_kernel,
        grid=(n // tm,),
        in_specs=[pl.BlockSpec((tm, D_MODEL), lambda i: (i, 0)), _const_spec((1, D_MODEL)),
                  _layer_spec(w_packed, layer), _const_spec((1, GATE_COLS))],
        out_specs=[pl.BlockSpec((tm, w), lambda i: (i, 0)) for w in PROJ_WIDTHS],
        out_shape=[jax.ShapeDtypeStruct((n, w), BF16) for w in PROJ_WIDTHS],
        compiler_params=_params(1),
        name="in_proj",
    )(x2, gain, w_packed, b_gate)


def _combine_in_proj_kernel(pos_ref, posn_ref, x_ref, route_ref, y_hbm, g_ref, w_ref, b_ref,
                            xo_ref, gate_ref, conv_ref, lat_ref, sb_ref, ybuf, sem, *, tm, n_steps):
    def consume(x, issue_part):
        xo_ref[...] = x
        _project(x, g_ref, w_ref, b_ref, gate_ref, conv_ref, lat_ref, sb_ref, PROJ_CHUNK, before_chunk=issue_part)

    _combine_steps(pos_ref, posn_ref, x_ref, route_ref, y_hbm, ybuf, sem, tc=tm, n_steps=n_steps, consume=consume,
                   spread_issue=True)


def _combine_in_proj(x_mid, route, y, pos, gain, w_packed, layer, b_gate, tm):
    n = x_mid.shape[0]
    n_steps = n // tm
    in_specs, scratch = _combine_specs(tm, n_steps)
    pos3 = pos.reshape(n_steps, 1, 2 * tm)
    widths = (D_MODEL,) + PROJ_WIDTHS
    return pl.pallas_call(
        functools.partial(_combine_in_proj_kernel, tm=tm, n_steps=n_steps),
        grid=(n_steps,),
        in_specs=in_specs + [_const_spec((1, D_MODEL)),
                             _layer_spec(w_packed, layer, pipeline_mode=pl.Buffered(1)),
                             _const_spec((1, GATE_COLS))],
        out_specs=[pl.BlockSpec((tm, w), lambda t: (t, 0)) for w in widths],
        out_shape=[jax.ShapeDtypeStruct((n, D_MODEL), F32)] + [jax.ShapeDtypeStruct((n, w), BF16)
                                                               for w in PROJ_WIDTHS],
        scratch_shapes=scratch,
        compiler_params=_params(1),
        name="combine_in_proj",
    )(pos3, pos3, x_mid, route, y, gain, w_packed, b_gate)


def _pack_w_in_kernel(w_ref, o_ref):
    rows = w_ref.shape[0]
    col = 0

    def put(values):
        nonlocal col
        o_ref[:, col:col + values.shape[1]] = values.astype(BF16)
        col += values.shape[1]

    put(w_ref[:, OFF_GATE:])
    put(w_ref[:, OFF_CONV:OFF_CQ])
    put(w_ref[:, OFF_CQ:OFF_KR])
    kr = w_ref[:, OFF_KR:OFF_SB]
    x1, x2 = kr[:, :ROPE_HALF], kr[:, ROPE_HALF:]
    z_lo = jnp.zeros((rows, MLA_NOPE), F32)
    z_hi = jnp.zeros((rows, HEAD_PAD - MLA_NOPE - MLA_ROPE), F32)
    put(jnp.concatenate([z_lo, x1, x2, z_hi], axis=1))
    put(jnp.concatenate([z_lo, -x2, x1, z_hi], axis=1))
    put(w_ref[:, OFF_SB:OFF_SB + SB_WIDTH] * (LOG2E * SB_HEAD_DIM ** -0.5))
    put(w_ref[:, OFF_SB + SB_WIDTH:OFF_GATE])
    assert col == o_ref.shape[1]


def _pack_w_in(w_in, rows=128):
    depth, d, cols = w_in.shape
    total = sum(PROJ_WIDTHS)
    return pl.pallas_call(
        _pack_w_in_kernel,
        grid=(depth, d // rows),
        in_specs=[pl.BlockSpec((None, rows, cols), lambda l, i: (l, i, 0))],
        out_specs=pl.BlockSpec((None, rows, total), lambda l, i: (l, i, 0)),
        out_shape=jax.ShapeDtypeStruct((depth, d, total), BF16),
        compiler_params=_params(2),
        name="pack_w_in",
    )(w_in)


def _mla_prep_kernel(lat_ref, cos_ref, sin_ref, qn_ref, kvn_ref, wqm_ref, wqs_ref, wk_ref, wv_ref,
                     q_ref, k_ref, v_ref, *, scale):
    cos, sin = cos_ref[...], sin_ref[...]
    cq = _rms(lat_ref[:, :MLA_Q_RANK].astype(F32), qn_ref[...]).astype(BF16)
    ckv = _rms(lat_ref[:, MLA_Q_RANK:MLA_Q_RANK + MLA_KV_RANK].astype(F32), kvn_ref[...]).astype(BF16)
    kr0 = MLA_Q_RANK + MLA_KV_RANK
    kr = (lat_ref[:, kr0:kr0 + HEAD_PAD].astype(F32) * cos
          + lat_ref[:, kr0 + HEAD_PAD:kr0 + 2 * HEAD_PAD].astype(F32) * sin)
    qm = jnp.dot(cq, wqm_ref[...], preferred_element_type=F32)
    qs = jnp.dot(cq, wqs_ref[...], preferred_element_type=F32)
    kn = jnp.dot(ckv, wk_ref[...], preferred_element_type=F32)
    for h in range(MLA_HEADS):
        sl = slice(h * HEAD_PAD, (h + 1) * HEAD_PAD)
        q_ref[:, sl] = ((qm[:, sl] * cos + qs[:, sl] * sin) * scale).astype(BF16)
        k_ref[:, sl] = (kn[:, sl] + kr).astype(BF16)
    v = jnp.dot(ckv, wv_ref[...], preferred_element_type=F32)
    lane = lax.broadcasted_iota(jnp.int32, v.shape, 1)
    v_ref[...] = jnp.where(lane % HEAD_PAD == MLA_V, 1.0, v).astype(BF16)


def _mla_prep(lat, cos, sin, q_norm, kv_norm, wq_main, wq_swap, wk, wv, tm):
    n = lat.shape[0]
    hw = MLA_HEADS * HEAD_PAD
    row = lambda w: pl.BlockSpec((tm, w), lambda i: (i, 0))
    return pl.pallas_call(
        functools.partial(_mla_prep_kernel, scale=LOG2E * (MLA_NOPE + MLA_ROPE) ** -0.5),
        grid=(n // tm,),
        in_specs=[row(LAT_COLS), row(HEAD_PAD), row(HEAD_PAD), _const_spec((1, MLA_Q_RANK)),
                  _const_spec((1, MLA_KV_RANK)), _const_spec((MLA_Q_RANK, hw)), _const_spec((MLA_Q_RANK, hw)),
                  _const_spec((MLA_KV_RANK, hw)), _const_spec((MLA_KV_RANK, hw))],
        out_specs=[row(hw), row(hw), row(hw)],
        out_shape=[jax.ShapeDtypeStruct((n, hw), BF16)] * 3,
        compiler_params=_params(1),
        name="mla_prep",
    )(lat, cos, sin, q_norm, kv_norm, wq_main, wq_swap, wk, wv)


def _pack_mla_weights(w_uq, w_ukv):
    qd = MLA_NOPE + MLA_ROPE
    z_hi = jnp.zeros((MLA_Q_RANK, HEAD_PAD - qd), w_uq.dtype)
    z_lo = jnp.zeros((MLA_Q_RANK, MLA_NOPE), w_uq.dtype)
    main, swap, wk, wv = [], [], [], []
    for h in range(MLA_HEADS):
        wq = w_uq[:, h * qd:(h + 1) * qd]
        nope, x1, x2 = wq[:, :MLA_NOPE], wq[:, MLA_NOPE:MLA_NOPE + ROPE_HALF], wq[:, MLA_NOPE + ROPE_HALF:]
        main += [nope, x1, x2, z_hi]
        swap += [z_lo, -x2, x1, z_hi]
        kv = w_ukv[:, h * (MLA_NOPE + MLA_V):(h + 1) * (MLA_NOPE + MLA_V)]
        wk += [kv[:, :MLA_NOPE], jnp.zeros((MLA_KV_RANK, HEAD_PAD - MLA_NOPE), w_ukv.dtype)]
        wv += [kv[:, MLA_NOPE:], jnp.zeros((MLA_KV_RANK, HEAD_PAD - MLA_V), w_ukv.dtype)]
    cat = lambda parts: jnp.concatenate(parts, axis=1).astype(BF16)
    return cat(main), cat(swap), cat(wk), cat(wv)


def _mla_attn_kernel(q_ref, k_ref, v_ref, o_ref, m_ref, acc_ref, *, tq):
    i = pl.program_id(1)
    row = lax.broadcasted_iota(jnp.int32, (tq, tq), 0)
    col = lax.broadcasted_iota(jnp.int32, (tq, tq), 1)
    causal = row >= col
    dn = (((1,), (1,)), ((), ()))

    def step(j, width, diagonal=False):
        start = pl.multiple_of(j * tq, tq)
        for h in range(MLA_HEADS):
            hs = slice(h * HEAD_PAD, (h + 1) * HEAD_PAD)
            s = lax.dot_general(q_ref[:, hs], k_ref[pl.ds(start, width), hs], dn, preferred_element_type=F32)
            if diagonal:
                s = jnp.where(causal, s, -jnp.inf)
            m_cur = jnp.max(s, axis=1, keepdims=True)
            if diagonal:
                m_new = jnp.broadcast_to(m_cur, (tq, LANES))
            else:
                m_old = m_ref[h]
                m_new = jnp.maximum(m_old, m_cur)
            p = jnp.exp2(s - jnp.concatenate([m_new] * (width // LANES), axis=1))
            pv = jnp.dot(p.astype(BF16), v_ref[pl.ds(start, width), hs], preferred_element_type=F32)
            if diagonal:
                acc_ref[h] = pv
            else:
                acc_ref[h] = jnp.exp2(m_old - m_new) * acc_ref[h] + pv
            m_ref[h] = m_new

    step(i, tq, diagonal=True)

    unroll = 4

    def body(jj, carry):
        for u in range(unroll):
            step(unroll * jj + u, tq)
        return carry

    lax.fori_loop(0, i // unroll, body, 0)

    rest = i - i % unroll

    @pl.when(i % unroll >= 2)
    def _():
        step(rest, tq)
        step(rest + 1, tq)

    @pl.when(i % 2 == 1)
    def _():
        step(i - 1, tq)

    outs = []
    for h in range(MLA_HEADS):
        acc = acc_ref[h]
        outs.append(acc[:, :MLA_V] / acc[:, MLA_V:MLA_V + 1])
    o_ref[...] = jnp.concatenate(outs, axis=1).astype(o_ref.dtype)


def _mla_attn(q, k, v, batch, seq, tq):
    n = q.shape[0]
    hw = MLA_HEADS * HEAD_PAD
    vw = MLA_HEADS * MLA_V
    nq = seq // tq
    return pl.pallas_call(
        functools.partial(_mla_attn_kernel, tq=tq),
        grid=(batch, nq),
        in_specs=[pl.BlockSpec((tq, hw), lambda b, i: (b * nq + i, 0)),
                  pl.BlockSpec((seq, hw), lambda b, i: (b, 0)),
                  pl.BlockSpec((seq, hw), lambda b, i: (b, 0))],
        out_specs=pl.BlockSpec((tq, vw), lambda b, i: (b * nq + i, 0)),
        out_shape=jax.ShapeDtypeStruct((n, vw), BF16),
        scratch_shapes=[pltpu.VMEM((MLA_HEADS, tq, LANES), F32), pltpu.VMEM((MLA_HEADS, tq, HEAD_PAD), F32)],
        compiler_params=_params(2),
        name="mla_attn",
    )(q, k, v)


def _sb_attn_kernel(q_ref, k_ref, v_ref, tri_ref, o_ref, run_ref, acc_ref, *, tq):
    i = pl.program_id(1)
    row = lax.broadcasted_iota(jnp.int32, (tq, tq), 0)
    col = lax.broadcasted_iota(jnp.int32, (tq, tq), 1)
    strict = row > col
    dn = (((1,), (1,)), ((), ()))
    tri = tri_ref[...]

    def step(j, diagonal):
        start = pl.multiple_of(j * tq, tq)
        for h in range(SB_HEADS):
            hs = slice(h * SB_HEAD_DIM, (h + 1) * SB_HEAD_DIM)
            z = lax.dot_general(q_ref[:, hs], k_ref[pl.ds(start, tq), hs], dn, preferred_element_type=F32)
            sp = jnp.maximum(z, 0.0) + jnp.log(1.0 + jnp.exp2(-jnp.abs(z))) * LOG2E
            if diagonal:
                sp = jnp.where(strict, sp, 0.0)
            later = jnp.dot(sp.astype(BF16), tri, preferred_element_type=F32)
            a = jnp.exp2(z - sp - later)
            if diagonal:
                a = jnp.where(strict, a, 0.0)
            av = jnp.dot(a.astype(BF16), v_ref[pl.ds(start, tq), hs], preferred_element_type=F32)
            total = jnp.sum(sp, axis=1, keepdims=True)
            if diagonal:
                acc_ref[h] = av
                run_ref[h] = jnp.broadcast_to(total, (tq, LANES))
            else:
                run = run_ref[h]
                acc_ref[h] += jnp.exp2(-run[:, :SB_HEAD_DIM]) * av
                run_ref[h] = run + total

    @pl.when(i == 0)
    def _():
        step(0, True)

    @pl.when(i > 0)
    def _():
        step(i, True)
        step(i - 1, False)

    def alive():
        return (jnp.min(run_ref[...]) < SB_DEAD).astype(jnp.int32)

    def cond(carry):
        t, live = carry
        return (t < i) & (live > 0)

    def body(carry):
        t, _ = carry
        step(i - 1 - t, False)
        return t + 1, alive()

    lax.while_loop(cond, body, (jnp.int32(1), alive()))
    o_ref[...] = jnp.concatenate([acc_ref[h] for h in range(SB_HEADS)], axis=1).astype(o_ref.dtype)


def _sb_attn(qkv, batch, seq, tq):
    n = qkv.shape[0]
    nq = seq // tq
    j = jnp.arange(tq)
    tri = (j[:, None] > j[None, :]).astype(BF16)
    return pl.pallas_call(
        functools.partial(_sb_attn_kernel, tq=tq),
        grid=(batch, nq),
        in_specs=[pl.BlockSpec((tq, SB_WIDTH), lambda b, i: (b * nq + i, 0)),
                  pl.BlockSpec((seq, SB_WIDTH), lambda b, i: (b, 1)),
                  pl.BlockSpec((seq, SB_WIDTH), lambda b, i: (b, 2)),
                  _const_spec((tq, tq))],
        out_specs=pl.BlockSpec((tq, SB_WIDTH), lambda b, i: (b * nq + i, 0)),
        out_shape=jax.ShapeDtypeStruct((n, SB_WIDTH), BF16),
        scratch_shapes=[pltpu.VMEM((SB_HEADS, tq, LANES), F32), pltpu.VMEM((SB_HEADS, tq, SB_HEAD_DIM), F32)],
        compiler_params=_params(2),
        name="sb_attn",
    )(qkv, qkv, qkv, tri)


def _merge_kernel(x_ref, gate_ref, conv_ref, halo_ref, omla_ref, osb_ref, cw_ref, wa_ref, wb_ref, wc_ref,
                  wo_ref, fn_ref, wr_cat_ref, br_ref, low_ref, xo_ref, h2_ref, route_ref, cnt_ref,
                  *, tm, sub, tiles_per_seq):
    i = pl.program_id(0)
    f = lambda r: r.astype(F32)
    cw = cw_ref[...]

    @pl.when(i == 0)
    def _():
        cnt_ref[...] = jnp.zeros_like(cnt_ref)

    counts = cnt_ref[...]
    for r0 in range(0, tm, sub):
        rows = slice(r0, r0 + sub)
        conv = conv_ref[rows, :]
        u = f(conv[:, 2 * CONV_WIDTH:]) * f(conv[:, :CONV_WIDTH])
        if r0 == 0:
            halo = halo_ref[...]
            up = f(halo[:, 2 * CONV_WIDTH:]) * f(halo[:, :CONV_WIDTH])
            up = jnp.where((i % tiles_per_seq) == 0, 0.0, up)
        else:
            halo = conv_ref[r0 - 8:r0, :]
            up = f(halo[:, 2 * CONV_WIDTH:]) * f(halo[:, :CONV_WIDTH])
        ue = jnp.concatenate([up, u], axis=0)
        y = cw[0:1, :] * ue[6:sub + 6] + cw[1:2, :] * ue[7:sub + 7] + cw[2:3, :] * u
        ya = (f(conv[:, CONV_WIDTH:2 * CONV_WIDTH]) * y).astype(BF16)
        gate = gate_ref[rows, :]
        merged = (f(gate[:, :D_MODEL]) * jnp.dot(ya, wa_ref[...], preferred_element_type=F32)
                  + f(gate[:, D_MODEL:2 * D_MODEL]) * jnp.dot(omla_ref[rows, :], wb_ref[...],
                                                              preferred_element_type=F32)
                  + f(gate[:, 2 * D_MODEL:]) * jnp.dot(osb_ref[rows, :], wc_ref[...], preferred_element_type=F32))
        x_new = x_ref[rows, :] + jnp.dot(merged.astype(BF16), wo_ref[...], preferred_element_type=F32)
        xo_ref[rows, :] = x_new
        h2 = _rms(x_new, fn_ref[...])
        h2_hi = h2.astype(BF16)
        for c in range(ROW_TILES):
            h2_ref[pl.ds(r0 * ROW_TILES + c, sub, stride=ROW_TILES), :] = (
                h2_hi[:, c * LANES:(c + 1) * LANES].astype(F32))
        h2_lo = (h2 - h2_hi.astype(F32)).astype(BF16)
        both = jnp.dot(h2_hi, wr_cat_ref[...], preferred_element_type=F32)
        logits = (both[:, :ROUTER_COLS] + both[:, ROUTER_COLS:]
                  + jnp.dot(h2_lo, wr_cat_ref[:, :ROUTER_COLS], preferred_element_type=F32)) + br_ref[...]
        w1, w2, i1, i2 = _route(logits)

        lane = lax.broadcasted_iota(jnp.int32, logits.shape, 1)
        pick1, pick2 = lane == i1, lane == i2
        onehot = jnp.where(pick1 | pick2, 1.0, 0.0)
        before = jnp.dot(low_ref[...], onehot.astype(BF16), preferred_element_type=F32) + counts
        rank1 = jnp.sum(jnp.where(pick1, before, 0.0), axis=-1, keepdims=True)
        rank2 = jnp.sum(jnp.where(pick2, before, 0.0), axis=-1, keepdims=True)
        counts = counts + jnp.sum(onehot, axis=0, keepdims=True)
        cols = (w1, w2, i1.astype(F32), i2.astype(F32), rank1, rank2)
        route = jnp.zeros(logits.shape, F32)
        for k, col in enumerate(cols):
            route = jnp.where(lane == k, col, route)
        route_ref[rows, :] = route
    cnt_ref[...] = counts


def _route(logits):
    lane = lax.broadcasted_iota(jnp.int32, logits.shape, 1)
    big = jnp.int32(1 << 30)
    neg = -jnp.inf
    is_group = (lane >= N_EXPERTS) & (lane < N_EXPERTS + N_GROUPS)
    gl = jnp.where(is_group, logits, neg)
    gmax = jnp.max(gl, axis=-1, keepdims=True)
    g_w = 1.0 / jnp.sum(jnp.exp(gl - gmax), axis=-1, keepdims=True)
    g_idx = jnp.min(jnp.where(gl == gmax, lane, big), axis=-1, keepdims=True) - N_EXPERTS
    lo = g_idx * EXPERTS_PER_GROUP
    el = jnp.where((lane >= lo) & (lane < lo + EXPERTS_PER_GROUP), logits, neg)
    m1 = jnp.max(el, axis=-1, keepdims=True)
    i1 = jnp.min(jnp.where(el == m1, lane, big), axis=-1, keepdims=True)
    el2 = jnp.where(lane == i1, neg, el)
    m2 = jnp.max(el2, axis=-1, keepdims=True)
    i2 = jnp.min(jnp.where(el2 == m2, lane, big), axis=-1, keepdims=True)
    r = jnp.exp(m2 - m1)
    w1 = g_w / (1.0 + r)
    w2 = g_w * r / (1.0 + r)
    return w1, w2, i1, i2


def _merge(x2, gates, conv, o_mla, o_sb, conv_w, wa, wb, wc, wo, ffn_norm, wr_cat, b_router, seq, tm):
    n = x2.shape[0]
    row = lambda w: pl.BlockSpec((tm, w), lambda i: (i, 0))
    halo = pl.BlockSpec((8, CONV_COLS), lambda i: (jnp.maximum(i * (tm // 8) - 1, 0), 0))
    sub = tm
    r = jnp.arange(sub)
    lower = (r[:, None] > r[None, :]).astype(BF16)
    return pl.pallas_call(
        functools.partial(_merge_kernel, tm=tm, sub=sub, tiles_per_seq=seq // tm),
        grid=(n // tm,),
        in_specs=[row(D_MODEL), row(GATE_COLS), row(CONV_COLS), halo, row(CONV_WIDTH), row(SB_WIDTH),
                  _const_spec((8, CONV_WIDTH)), _const_spec((CONV_WIDTH, D_MODEL)),
                  _const_spec((MLA_HEADS * MLA_V, D_MODEL)), _const_spec((SB_WIDTH, D_MODEL)),
                  _const_spec((D_MODEL, D_MODEL)), _const_spec((1, D_MODEL)),
                  _const_spec((D_MODEL, 2 * ROUTER_COLS)),
                  _const_spec((1, ROUTER_COLS)), _const_spec((sub, sub))],
        out_specs=[row(D_MODEL), pl.BlockSpec((tm * ROW_TILES, LANES), lambda i: (i, 0)), row(ROUTER_COLS),
                   _const_spec((1, ROUTER_COLS))],
        out_shape=[jax.ShapeDtypeStruct((n, D_MODEL), F32), jax.ShapeDtypeStruct((n * ROW_TILES, LANES), F32),
                   jax.ShapeDtypeStruct((n, ROUTER_COLS), F32), jax.ShapeDtypeStruct((1, ROUTER_COLS), F32)],
        compiler_params=_params(1),
        name="merge",
    )(x2, gates, conv, conv, o_mla, o_sb, conv_w, wa, wb, wc, wo, ffn_norm, wr_cat, b_router, lower)


def _pack_router(w_rg, b_rg, w_re, b_re):
    pad = ROUTER_COLS - N_EXPERTS - N_GROUPS
    w = jnp.concatenate([w_re, w_rg, jnp.zeros((D_MODEL, pad), F32)], axis=1)
    b = jnp.concatenate([b_re, b_rg, jnp.zeros((pad,), F32)])[None, :]
    hi = w.astype(BF16)
    lo = (w - hi.astype(F32)).astype(BF16)
    return jnp.concatenate([hi, lo], axis=1), b


def _moe_plan(route, counts, n):
    cnt = counts[0, :N_EXPERTS].astype(jnp.int32)
    padded = (cnt + MOE_TILE - 1) // MOE_TILE * MOE_TILE
    ends = jnp.cumsum(padded)
    starts = ends - padded
    n_used = ends[-1] // MOE_TILE
    expert = route[:, 2:4].astype(jnp.int32)
    rank = route[:, 4:6].astype(jnp.int32)
    chosen = expert[..., None] == jnp.arange(N_EXPERTS, dtype=jnp.int32)
    pos = jnp.sum(jnp.where(chosen, starts, 0), axis=-1) + rank
    n_tiles = (2 * n) // MOE_TILE + N_EXPERTS
    tile_start = jnp.minimum(jnp.arange(n_tiles, dtype=jnp.int32), n_used - 1) * MOE_TILE
    tile_expert = jnp.sum((ends[None, :] <= tile_start[:, None]).astype(jnp.int32), axis=1)
    last_tile = jnp.where(padded > 0, ends // MOE_TILE - 1, -1)
    tail = n_used + jnp.arange(N_EXPERTS, dtype=jnp.int32)
    zero_tiles = jnp.concatenate([last_tile, jnp.where(tail < n_tiles, tail, -1)])
    return pos, tile_expert, n_used.reshape(1), zero_tiles, n_tiles


def _dispatch_kernel(zt_ref, pos_ref, h_ref, xs_hbm, zero_ref, sem, zsem, *, td):
    t = pl.program_id(0)

    def fill(j):
        return pltpu.make_async_copy(zero_ref, _token_rows(xs_hbm, zt_ref[j] * MOE_TILE, MOE_TILE), zsem)

    @pl.when(t == 0)
    def _():
        zero_ref[...] = jnp.zeros_like(zero_ref)
        for j in range(2 * N_EXPERTS):
            @pl.when(zt_ref[j] >= 0)
            def _():
                fill(j).start()
        for j in range(2 * N_EXPERTS):
            @pl.when(zt_ref[j] >= 0)
            def _():
                fill(j).wait()

    for r in range(td):
        src = h_ref.at[pl.ds(r * ROW_TILES, ROW_TILES)]
        for k in range(2):
            pltpu.make_async_copy(src, _token_rows(xs_hbm, pos_ref[0, 0, 2 * r + k]), sem).start()

    for k in range(2):
        pltpu.make_async_copy(h_ref, _token_rows(xs_hbm, 0, td), sem).wait()


def _dispatch(h2, pos3, zero_tiles, n_tiles, td):
    n = h2.shape[0] // ROW_TILES
    n_steps = n // td
    return pl.pallas_call(
        functools.partial(_dispatch_kernel, td=td),
        grid_spec=pltpu.PrefetchScalarGridSpec(
            num_scalar_prefetch=1, grid=(n_steps,),
            in_specs=[pl.BlockSpec((1, 1, 2 * td), lambda t, zt: (t, 0, 0), memory_space=pltpu.SMEM),
                      pl.BlockSpec((td * ROW_TILES, LANES), lambda t, zt: (t, 0))],
            out_specs=pl.BlockSpec(memory_space=pl.ANY),
            scratch_shapes=[pltpu.VMEM((MOE_TILE * ROW_TILES, LANES), F32), pltpu.SemaphoreType.DMA,
                            pltpu.SemaphoreType.DMA]),
        out_shape=jax.ShapeDtypeStruct((n_tiles * MOE_TILE * ROW_TILES, LANES), F32),
        compiler_params=_params(1),
        name="moe_dispatch",
    )(zero_tiles, pos3, h2)


def _experts_kernel(te_ref, nu_ref, xs_ref, wg_ref, wu_ref, wd_ref, y_ref, wg_bf, wu_bf, wd_bf):
    t = pl.program_id(0)

    @pl.when(t >= nu_ref[0])
    def _():
        y_ref[...] = jnp.zeros_like(y_ref)

    @pl.when((t == 0) | (te_ref[t] != te_ref[jnp.maximum(t - 1, 0)]))
    def _():
        wg_bf[...] = wg_ref[0].astype(BF16)
        wu_bf[...] = wu_ref[0].astype(BF16)
        wd_bf[...] = wd_ref[0].astype(BF16)

    @pl.when(t < nu_ref[0])
    def _():
        x = jnp.concatenate([xs_ref[pl.ds(c, MOE_TILE, stride=ROW_TILES), :] for c in range(ROW_TILES)],
                            axis=1).astype(BF16)
        g = jnp.dot(x, wg_bf[...], preferred_element_type=F32)
        u = jnp.dot(x, wu_bf[...], preferred_element_type=F32)
        act = (g * jax.nn.sigmoid(g) * u).astype(BF16)
        y = jnp.dot(act, wd_bf[...], preferred_element_type=F32)
        for c in range(ROW_TILES):
            y_ref[pl.ds(c, MOE_TILE, stride=ROW_TILES), :] = y[:, c * LANES:(c + 1) * LANES]


def _experts(xs, tile_expert, n_used, wg, wu, wd, layer, n_tiles):
    shape = (MOE_TILE * ROW_TILES, LANES)
    rows_in = pl.BlockSpec(shape, lambda t, te, nu: (jnp.minimum(t, nu[0] - 1), 0))
    rows_out = pl.BlockSpec(shape, lambda t, te, nu: (t, 0))
    weight = lambda shape: pl.BlockSpec((1,) + shape, lambda t, te, nu: (layer * N_EXPERTS + te[t], 0, 0))
    up, down = (D_MODEL, D_EXPERT), (D_EXPERT, D_MODEL)
    return pl.pallas_call(
        _experts_kernel,
        grid_spec=pltpu.PrefetchScalarGridSpec(
            num_scalar_prefetch=2, grid=(n_tiles,),
            in_specs=[rows_in, weight(up), weight(up), weight(down)],
            out_specs=rows_out,
            scratch_shapes=[pltpu.VMEM(up, BF16), pltpu.VMEM(up, BF16), pltpu.VMEM(down, BF16)]),
        out_shape=jax.ShapeDtypeStruct(xs.shape, F32),
        compiler_params=_params(1),
        name="moe_experts",
    )(tile_expert, n_used, xs, wg, wu, wd)


def _combine_norm_kernel(pos_ref, posn_ref, x_ref, route_ref, y_hbm, fin_ref, o_ref, ybuf, sem, *, tc, n_steps):
    def consume(x, issue_part):
        o_ref[...] = _rms(x, fin_ref[...])

    _combine_steps(pos_ref, posn_ref, x_ref, route_ref, y_hbm, ybuf, sem, tc=tc, n_steps=n_steps, consume=consume)


def _combine_norm(x_mid, route, y, pos, final_norm, tc):
    n = x_mid.shape[0]
    n_steps = n // tc
    in_specs, scratch = _combine_specs(tc, n_steps)
    pos3 = pos.reshape(n_steps, 1, 2 * tc)
    return pl.pallas_call(
        functools.partial(_combine_norm_kernel, tc=tc, n_steps=n_steps),
        grid=(n_steps,),
        in_specs=in_specs + [_const_spec((1, D_MODEL))],
        out_specs=pl.BlockSpec((tc, D_MODEL), lambda t: (t, 0)),
        out_shape=jax.ShapeDtypeStruct((n, D_MODEL), F32),
        scratch_shapes=scratch,
        compiler_params=_params(1),
        name="moe_combine_norm",
    )(pos3, pos3, x_mid, route, y, final_norm)


def _moe_experts(h2, route, counts, wg, wu, wd, layer):
    n = route.shape[0]
    pos, tile_expert, n_used, zero_tiles, n_tiles = _moe_plan(route, counts, n)
    td = _tile(n, 512)
    xs = _dispatch(h2, pos.reshape(n // td, 1, 2 * td), zero_tiles, n_tiles, td)
    return _experts(xs, tile_expert, n_used, wg, wu, wd, layer, n_tiles), pos


def _tile(n, pref):
    t = min(n, pref)
    assert n % t == 0, (n, t)
    return t


def kernel(x, positions, attn_norm, w_in, b_gate, conv_w, w_out_conv, q_norm, kv_norm, w_uq, w_ukv, w_out_mla, w_out_sb, w_o, ffn_norm, w_router_group, b_router_group, w_router_expert, b_router_expert, w_exp_gate, w_exp_up, w_exp_down, final_norm):
    batch, seq, d = x.shape
    assert d == D_MODEL
    n = batch * seq
    depth = w_in.shape[0]
    tm_rows = _tile(seq, 512)
    tq = _tile(seq, 256)
    x2 = x.reshape(n, d)
    cos, sin = _rope_tables(positions, _tile(n, 1024))
    wg_all = w_exp_gate.reshape(depth * N_EXPERTS, D_MODEL, D_EXPERT)
    wu_all = w_exp_up.reshape(depth * N_EXPERTS, D_MODEL, D_EXPERT)
    wd_all = w_exp_down.reshape(depth * N_EXPERTS, D_EXPERT, D_MODEL)
    w_in_packed = _pack_w_in(w_in)
    moe = None
    for l in range(depth):
        proj_args = (attn_norm[l][None, :], w_in_packed, l, b_gate[l][None, :], tm_rows)
        if moe is None:
            gates, conv, lat, sb = _in_proj(x2, *proj_args)
        else:
            x2, gates, conv, lat, sb = _combine_in_proj(*moe, *proj_args)
        wq_main, wq_swap, wk, wv = _pack_mla_weights(w_uq[l], w_ukv[l])
        q, k, v = _mla_prep(lat, cos, sin, q_norm[l][None, :], kv_norm[l][None, :], wq_main, wq_swap, wk, wv,
                            tm_rows)
        o_mla = _mla_attn(q, k, v, batch, seq, tq)
        o_sb = _sb_attn(sb, batch, seq, tq)
        wr_cat, b_router = _pack_router(w_router_group[l], b_router_group[l], w_router_expert[l],
                                        b_router_expert[l])
        conv_w8 = jnp.concatenate([conv_w[l], jnp.zeros((8 - CONV_K, CONV_WIDTH), F32)], axis=0)
        x_mid, h2, route, counts = _merge(x2, gates, conv, o_mla, o_sb, conv_w8, w_out_conv[l].astype(BF16),
                                          w_out_mla[l].astype(BF16), w_out_sb[l].astype(BF16),
                                          w_o[l].astype(BF16), ffn_norm[l][None, :], wr_cat, b_router, seq,
                                          tm_rows)
        y, pos = _moe_experts(h2, route, counts, wg_all, wu_all, wd_all, l)
        moe = (x_mid, route, y, pos)
    out = _combine_norm(*moe, final_norm[None, :], _tile(n, 256))
    return out.reshape(batch, seq, d)
```

```python
import functools

import jax
import jax.numpy as jnp
from jax import lax
from jax.experimental import pallas as pl
from jax.experimental.pallas import tpu as pltpu

D_MODEL = 1024
CONV_WIDTH = 512
CONV_K = 3
MLA_HEADS = 8
MLA_NOPE = 64
MLA_ROPE = 32
MLA_V = 64
MLA_Q_RANK = 256
MLA_KV_RANK = 128
ROPE_THETA = 10000.0
SB_HEADS = 8
SB_HEAD_DIM = 64
SB_WIDTH = SB_HEADS * SB_HEAD_DIM
N_BRANCHES = 3
OFF_CONV = 0
OFF_CQ = OFF_CONV + 3 * CONV_WIDTH
OFF_CKV = OFF_CQ + MLA_Q_RANK
OFF_KR = OFF_CKV + MLA_KV_RANK
OFF_SB = OFF_KR + MLA_ROPE
OFF_GATE = OFF_SB + 3 * SB_WIDTH
N_GROUPS = 4
EXPERTS_PER_GROUP = 8
N_EXPERTS = N_GROUPS * EXPERTS_PER_GROUP
D_EXPERT = 256
EPS = 1e-6
LOG2E = 1.4426950408889634
SB_DEAD = 160.0

LANES = 128
HEAD_PAD = 128
ROPE_HALF = MLA_ROPE // 2
GATE_COLS = N_BRANCHES * D_MODEL
CONV_COLS = 3 * CONV_WIDTH
LAT_COLS = MLA_Q_RANK + MLA_KV_RANK + 2 * HEAD_PAD
SB_COLS = 3 * SB_WIDTH
ROUTER_COLS = LANES
ROW_TILES = D_MODEL // LANES
MOE_TILE = 256
VMEM_LIMIT = 56 * 1024 * 1024

BF16 = jnp.bfloat16
F32 = jnp.float32


def _params(n_axes, vmem=VMEM_LIMIT):
    return pltpu.CompilerParams(dimension_semantics=("arbitrary",) * n_axes, vmem_limit_bytes=vmem)


def _rms(xf, gain):
    return xf * lax.rsqrt(jnp.mean(xf * xf, axis=-1, keepdims=True) + EPS) * gain


def _const_spec(shape):
    return pl.BlockSpec(shape, lambda *_: (0,) * len(shape))


def _rope_table_kernel(pos_ref, freq_ref, cos_ref, sin_ref):
    ang = pos_ref[...].astype(F32) * freq_ref[...]
    cos_ref[...] = jnp.cos(ang)
    sin_ref[...] = jnp.sin(ang)


def _rope_tables(positions, tm):
    n = positions.size
    half = ROPE_HALF
    freqs = ROPE_THETA ** (-jnp.arange(half, dtype=F32) / half)
    zeros = jnp.zeros((MLA_NOPE,), F32)
    freq_row = jnp.concatenate([zeros, freqs, freqs, jnp.zeros((HEAD_PAD - MLA_NOPE - MLA_ROPE,), F32)])[None, :]
    pos = positions.reshape(n, 1)
    return pl.pallas_call(
        _rope_table_kernel,
        grid=(n // tm,),
        in_specs=[pl.BlockSpec((tm, 1), lambda i: (i, 0)), _const_spec((1, HEAD_PAD))],
        out_specs=[pl.BlockSpec((tm, HEAD_PAD), lambda i: (i, 0))] * 2,
        out_shape=[jax.ShapeDtypeStruct((n, HEAD_PAD), F32)] * 2,
        compiler_params=_params(1),
        name="rope_tables",
    )(pos, freq_row)


def _token_rows(ref, first_token, count=1):
    start = pl.multiple_of(first_token * ROW_TILES, ROW_TILES)
    return ref.at[pl.ds(start, count * ROW_TILES)]


def _combine_steps(pos_ref, posn_ref, x_ref, route_ref, y_hbm, ybuf, sem, *, tc, n_steps, consume,
                   spread_issue=False):
    t = pl.program_id(0)

    def issue(p_ref, slot, part=0, parts=1):
        for r in range(2 * tc * part // parts, 2 * tc * (part + 1) // parts):
            pltpu.make_async_copy(_token_rows(y_hbm, p_ref[0, 0, r]),
                                  ybuf.at[slot, pl.ds(r * ROW_TILES, ROW_TILES)],
                                  sem.at[slot]).start(priority=0 if spread_issue else r % 2)

    def wait(slot):
        pltpu.make_async_copy(_token_rows(y_hbm, 0, 2 * tc), ybuf.at[slot], sem.at[slot]).wait()

    @pl.when(t == 0)
    def _():
        issue(pos_ref, 0)

    def run(slot):
        if not spread_issue:
            issue(posn_ref, 1 - slot)
        wait(slot)
        w1, w2 = route_ref[:, 0:1], route_ref[:, 1:2]
        outs = []
        for c in range(ROW_TILES):
            y1 = ybuf[slot, pl.ds(c, tc, stride=2 * ROW_TILES), :]
            y2 = ybuf[slot, pl.ds(ROW_TILES + c, tc, stride=2 * ROW_TILES), :]
            outs.append(x_ref[:, c * LANES:(c + 1) * LANES] + w1 * y1 + w2 * y2)
        issue_part = functools.partial(issue, posn_ref, 1 - slot) if spread_issue else None
        consume(jnp.concatenate(outs, axis=1), issue_part)

    for slot in range(2):
        @pl.when(t % 2 == slot)
        def _():
            run(slot)

    @pl.when(t == n_steps - 1)
    def _():
        wait(n_steps % 2)


def _combine_specs(tc, n_steps):
    pos_spec = lambda step: pl.BlockSpec((1, 1, 2 * tc), lambda t: (step(t), 0, 0), memory_space=pltpu.SMEM)
    row = lambda w: pl.BlockSpec((tc, w), lambda t: (t, 0))
    in_specs = [pos_spec(lambda t: t), pos_spec(lambda t: jnp.minimum(t + 1, n_steps - 1)), row(D_MODEL),
                row(ROUTER_COLS), pl.BlockSpec(memory_space=pl.ANY)]
    scratch = [pltpu.VMEM((2, 2 * tc * ROW_TILES, LANES), F32), pltpu.SemaphoreType.DMA((2,))]
    return in_specs, scratch


def _project(x, g_ref, w_ref, b_ref, gate_ref, conv_ref, lat_ref, sb_ref, chunk, before_chunk=None):
    h = _rms(x, g_ref[...]).astype(BF16)
    n_chunks = sum(-(-w // chunk) for w in PROJ_WIDTHS)
    done = [0]

    def run(out_ref, col0, width, epilogue):
        for c in range(0, width, chunk):
            cw = min(chunk, width - c)
            if before_chunk is not None:
                before_chunk(done[0], n_chunks)
            done[0] += 1
            acc = jnp.dot(h, w_ref[:, col0 + c:col0 + c + cw], preferred_element_type=F32)
            out_ref[:, c:c + cw] = epilogue(acc, c, cw).astype(out_ref.dtype)

    run(gate_ref, 0, GATE_COLS, lambda a, c, cw: jax.nn.sigmoid(a + b_ref[:, c:c + cw]))
    ident = lambda a, c, cw: a
    run(conv_ref, GATE_COLS, CONV_COLS, ident)
    run(lat_ref, GATE_COLS + CONV_COLS, LAT_COLS, ident)
    run(sb_ref, GATE_COLS + CONV_COLS + LAT_COLS, SB_COLS, ident)


PROJ_WIDTHS = (GATE_COLS, CONV_COLS, LAT_COLS, SB_COLS)
PROJ_CHUNK = 512


def _in_proj_kernel(x_ref, g_ref, w_ref, b_ref, gate_ref, conv_ref, lat_ref, sb_ref):
    _project(x_ref[...], g_ref, w_ref, b_ref, gate_ref, conv_ref, lat_ref, sb_ref, PROJ_CHUNK)


def _layer_spec(stacked, layer, **kwargs):
    shape = stacked.shape[1:]
    return pl.BlockSpec((None,) + shape, lambda *_: (layer,) + (0,) * len(shape), **kwargs)


def _in_proj(x2, gain, w_packed, layer, b_gate, tm):
    n = x2.shape[0]
    return pl.pallas_call(
        _in_proj_kernel,
        grid=(n // tm,),
        in_specs=[pl.BlockSpec((tm, D_MODEL), lambda i: (i, 0)), _const_spec((1, D_MODEL)),
                  _layer_spec(w_packed, layer), _const_spec((1, GATE_COLS))],
        out_specs=[pl.BlockSpec((tm, w), lambda i: (i, 0)) for w in PROJ_WIDTHS],
        out_shape=[jax.ShapeDtypeStruct((n, w), BF16) for w in PROJ_WIDTHS],
        compiler_params=_params(1),
        name="in_proj",
    )(x2, gain, w_packed, b_gate)


def _combine_in_proj_kernel(pos_ref, posn_ref, x_ref, route_ref, y_hbm, g_ref, w_ref, b_ref,
                            xo_ref, gate_ref, conv_ref, lat_ref, sb_ref, ybuf, sem, *, tm, n_steps):
    def consume(x, issue_part):
        xo_ref[...] = x
        _project(x, g_ref, w_ref, b_ref, gate_ref, conv_ref, lat_ref, sb_ref, PROJ_CHUNK, before_chunk=issue_part)

    _combine_steps(pos_ref, posn_ref, x_ref, route_ref, y_hbm, ybuf, sem, tc=tm, n_steps=n_steps, consume=consume,
                   spread_issue=True)


def _combine_in_proj(x_mid, route, y, pos, gain, w_packed, layer, b_gate, tm):
    n = x_mid.shape[0]
    n_steps = n // tm
    in_specs, scratch = _combine_specs(tm, n_steps)
    pos3 = pos.reshape(n_steps, 1, 2 * tm)
    widths = (D_MODEL,) + PROJ_WIDTHS
    return pl.pallas_call(
        functools.partial(_combine_in_proj_kernel, tm=tm, n_steps=n_steps),
        grid=(n_steps,),
        in_specs=in_specs + [_const_spec((1, D_MODEL)),
                             _layer_spec(w_packed, layer, pipeline_mode=pl.Buffered(1)),
                             _const_spec((1, GATE_COLS))],
        out_specs=[pl.BlockSpec((tm, w), lambda t: (t, 0)) for w in widths],
        out_shape=[jax.ShapeDtypeStruct((n, D_MODEL), F32)] + [jax.ShapeDtypeStruct((n, w), BF16)
                                                               for w in PROJ_WIDTHS],
        scratch_shapes=scratch,
        compiler_params=_params(1),
        name="combine_in_proj",
    )(pos3, pos3, x_mid, route, y, gain, w_packed, b_gate)


def _pack_w_in_kernel(w_ref, o_ref):
    rows = w_ref.shape[0]
    col = 0

    def put(values):
        nonlocal col
        o_ref[:, col:col + values.shape[1]] = values.astype(BF16)
        col += values.shape[1]

    put(w_ref[:, OFF_GATE:])
    put(w_ref[:, OFF_CONV:OFF_CQ])
    put(w_ref[:, OFF_CQ:OFF_KR])
    kr = w_ref[:, OFF_KR:OFF_SB]
    x1, x2 = kr[:, :ROPE_HALF], kr[:, ROPE_HALF:]
    z_lo = jnp.zeros((rows, MLA_NOPE), F32)
    z_hi = jnp.zeros((rows, HEAD_PAD - MLA_NOPE - MLA_ROPE), F32)
    put(jnp.concatenate([z_lo, x1, x2, z_hi], axis=1))
    put(jnp.concatenate([z_lo, -x2, x1, z_hi], axis=1))
    put(w_ref[:, OFF_SB:OFF_SB + SB_WIDTH] * (LOG2E * SB_HEAD_DIM ** -0.5))
    put(w_ref[:, OFF_SB + SB_WIDTH:OFF_GATE])
    assert col == o_ref.shape[1]


def _pack_w_in(w_in, rows=128):
    depth, d, cols = w_in.shape
    total = sum(PROJ_WIDTHS)
    return pl.pallas_call(
        _pack_w_in_kernel,
        grid=(depth, d // rows),
        in_specs=[pl.BlockSpec((None, rows, cols), lambda l, i: (l, i, 0))],
        out_specs=pl.BlockSpec((None, rows, total), lambda l, i: (l, i, 0)),
        out_shape=jax.ShapeDtypeStruct((depth, d, total), BF16),
        compiler_params=_params(2),
        name="pack_w_in",
    )(w_in)


def _mla_prep_kernel(lat_ref, cos_ref, sin_ref, qn_ref, kvn_ref, wqm_ref, wqs_ref, wk_ref, wv_ref,
                     q_ref, k_ref, v_ref, *, scale):
    cos, sin = cos_ref[...], sin_ref[...]
    cq = _rms(lat_ref[:, :MLA_Q_RANK].astype(F32), qn_ref[...]).astype(BF16)
    ckv = _rms(lat_ref[:, MLA_Q_RANK:MLA_Q_RANK + MLA_KV_RANK].astype(F32), kvn_ref[...]).astype(BF16)
    kr0 = MLA_Q_RANK + MLA_KV_RANK
    kr = (lat_ref[:, kr0:kr0 + HEAD_PAD].astype(F32) * cos
          + lat_ref[:, kr0 + HEAD_PAD:kr0 + 2 * HEAD_PAD].astype(F32) * sin)
    qm = jnp.dot(cq, wqm_ref[...], preferred_element_type=F32)
    qs = jnp.dot(cq, wqs_ref[...], preferred_element_type=F32)
    kn = jnp.dot(ckv, wk_ref[...], preferred_element_type=F32)
    for h in range(MLA_HEADS):
        sl = slice(h * HEAD_PAD, (h + 1) * HEAD_PAD)
        q_ref[:, sl] = ((qm[:, sl] * cos + qs[:, sl] * sin) * scale).astype(BF16)
        k_ref[:, sl] = (kn[:, sl] + kr).astype(BF16)
    v = jnp.dot(ckv, wv_ref[...], preferred_element_type=F32)
    lane = lax.broadcasted_iota(jnp.int32, v.shape, 1)
    v_ref[...] = jnp.where(lane % HEAD_PAD == MLA_V, 1.0, v).astype(BF16)


def _mla_prep(lat, cos, sin, q_norm, kv_norm, wq_main, wq_swap, wk, wv, tm):
    n = lat.shape[0]
    hw = MLA_HEADS * HEAD_PAD
    row = lambda w: pl.BlockSpec((tm, w), lambda i: (i, 0))
    return pl.pallas_call(
        functools.partial(_mla_prep_kernel, scale=LOG2E * (MLA_NOPE + MLA_ROPE) ** -0.5),
        grid=(n // tm,),
        in_specs=[row(LAT_COLS), row(HEAD_PAD), row(HEAD_PAD), _const_spec((1, MLA_Q_RANK)),
                  _const_spec((1, MLA_KV_RANK)), _const_spec((MLA_Q_RANK, hw)), _const_spec((MLA_Q_RANK, hw)),
                  _const_spec((MLA_KV_RANK, hw)), _const_spec((MLA_KV_RANK, hw))],
        out_specs=[row(hw), row(hw), row(hw)],
        out_shape=[jax.ShapeDtypeStruct((n, hw), BF16)] * 3,
        compiler_params=_params(1),
        name="mla_prep",
    )(lat, cos, sin, q_norm, kv_norm, wq_main, wq_swap, wk, wv)


def _pack_mla_weights(w_uq, w_ukv):
    qd = MLA_NOPE + MLA_ROPE
    z_hi = jnp.zeros((MLA_Q_RANK, HEAD_PAD - qd), w_uq.dtype)
    z_lo = jnp.zeros((MLA_Q_RANK, MLA_NOPE), w_uq.dtype)
    main, swap, wk, wv = [], [], [], []
    for h in range(MLA_HEADS):
        wq = w_uq[:, h * qd:(h + 1) * qd]
        nope, x1, x2 = wq[:, :MLA_NOPE], wq[:, MLA_NOPE:MLA_NOPE + ROPE_HALF], wq[:, MLA_NOPE + ROPE_HALF:]
        main += [nope, x1, x2, z_hi]
        swap += [z_lo, -x2, x1, z_hi]
        kv = w_ukv[:, h * (MLA_NOPE + MLA_V):(h + 1) * (MLA_NOPE + MLA_V)]
        wk += [kv[:, :MLA_NOPE], jnp.zeros((MLA_KV_RANK, HEAD_PAD - MLA_NOPE), w_ukv.dtype)]
        wv += [kv[:, MLA_NOPE:], jnp.zeros((MLA_KV_RANK, HEAD_PAD - MLA_V), w_ukv.dtype)]
    cat = lambda parts: jnp.concatenate(parts, axis=1).astype(BF16)
    return cat(main), cat(swap), cat(wk), cat(wv)


def _mla_attn_kernel(q_ref, k_ref, v_ref, o_ref, m_ref, acc_ref, *, tq):
    i = pl.program_id(1)
    row = lax.broadcasted_iota(jnp.int32, (tq, tq), 0)
    col = lax.broadcasted_iota(jnp.int32, (tq, tq), 1)
    causal = row >= col
    dn = (((1,), (1,)), ((), ()))

    def step(j, width, diagonal=False):
        start = pl.multiple_of(j * tq, tq)
        for h in range(MLA_HEADS):
            hs = slice(h * HEAD_PAD, (h + 1) * HEAD_PAD)
            s = lax.dot_general(q_ref[:, hs], k_ref[pl.ds(start, width), hs], dn, preferred_element_type=F32)
            if diagonal:
                s = jnp.where(causal, s, -jnp.inf)
            m_cur = jnp.max(s, axis=1, keepdims=True)
            if diagonal:
                m_new = jnp.broadcast_to(m_cur, (tq, LANES))
            else:
                m_old = m_ref[h]
                m_new = jnp.maximum(m_old, m_cur)
            p = jnp.exp2(s - jnp.concatenate([m_new] * (width // LANES), axis=1))
            pv = jnp.dot(p.astype(BF16), v_ref[pl.ds(start, width), hs], preferred_element_type=F32)
            if diagonal:
                acc_ref[h] = pv
            else:
                acc_ref[h] = jnp.exp2(m_old - m_new) * acc_ref[h] + pv
            m_ref[h] = m_new

    step(i, tq, diagonal=True)

    unroll = 4

    def body(jj, carry):
        for u in range(unroll):
            step(unroll * jj + u, tq)
        return carry

    lax.fori_loop(0, i // unroll, body, 0)

    rest = i - i % unroll

    @pl.when(i % unroll >= 2)
    def _():
        step(rest, tq)
        step(rest + 1, tq)

    @pl.when(i % 2 == 1)
    def _():
        step(i - 1, tq)

    outs = []
    for h in range(MLA_HEADS):
        acc = acc_ref[h]
        outs.append(acc[:, :MLA_V] / acc[:, MLA_V:MLA_V + 1])
    o_ref[...] = jnp.concatenate(outs, axis=1).astype(o_ref.dtype)


def _mla_attn(q, k, v, batch, seq, tq):
    n = q.shape[0]
    hw = MLA_HEADS * HEAD_PAD
    vw = MLA_HEADS * MLA_V
    nq = seq // tq
    return pl.pallas_call(
        functools.partial(_mla_attn_kernel, tq=tq),
        grid=(batch, nq),
        in_specs=[pl.BlockSpec((tq, hw), lambda b, i: (b * nq + i, 0)),
                  pl.BlockSpec((seq, hw), lambda b, i: (b, 0)),
                  pl.BlockSpec((seq, hw), lambda b, i: (b, 0))],
        out_specs=pl.BlockSpec((tq, vw), lambda b, i: (b * nq + i, 0)),
        out_shape=jax.ShapeDtypeStruct((n, vw), BF16),
        scratch_shapes=[pltpu.VMEM((MLA_HEADS, tq, LANES), F32), pltpu.VMEM((MLA_HEADS, tq, HEAD_PAD), F32)],
        compiler_params=_params(2),
        name="mla_attn",
    )(q, k, v)


def _sb_attn_kernel(q_ref, k_ref, v_ref, tri_ref, o_ref, run_ref, acc_ref, *, tq):
    i = pl.program_id(1)
    row = lax.broadcasted_iota(jnp.int32, (tq, tq), 0)
    col = lax.broadcasted_iota(jnp.int32, (tq, tq), 1)
    strict = row > col
    dn = (((1,), (1,)), ((), ()))
    tri = tri_ref[...]

    def step(j, diagonal):
        start = pl.multiple_of(j * tq, tq)
        for h in range(SB_HEADS):
            hs = slice(h * SB_HEAD_DIM, (h + 1) * SB_HEAD_DIM)
            z = lax.dot_general(q_ref[:, hs], k_ref[pl.ds(start, tq), hs], dn, preferred_element_type=F32)
            sp = jnp.maximum(z, 0.0) + jnp.log(1.0 + jnp.exp2(-jnp.abs(z))) * LOG2E
            if diagonal:
                sp = jnp.where(strict, sp, 0.0)
            later = jnp.dot(sp.astype(BF16), tri, preferred_element_type=F32)
            a = jnp.exp2(z - sp - later)
            if diagonal:
                a = jnp.where(strict, a, 0.0)
            av = jnp.dot(a.astype(BF16), v_ref[pl.ds(start, tq), hs], preferred_element_type=F32)
            total = jnp.sum(sp, axis=1, keepdims=True)
            if diagonal:
                acc_ref[h] = av
                run_ref[h] = jnp.broadcast_to(total, (tq, LANES))
            else:
                run = run_ref[h]
                acc_ref[h] += jnp.exp2(-run[:, :SB_HEAD_DIM]) * av
                run_ref[h] = run + total

    @pl.when(i == 0)
    def _():
        step(0, True)

    @pl.when(i > 0)
    def _():
        step(i, True)
        step(i - 1, False)

    def alive():
        return (jnp.min(run_ref[...]) < SB_DEAD).astype(jnp.int32)

    def cond(carry):
        t, live = carry
        return (t < i) & (live > 0)

    def body(carry):
        t, _ = carry
        step(i - 1 - t, False)
        return t + 1, alive()

    lax.while_loop(cond, body, (jnp.int32(1), alive()))
    o_ref[...] = jnp.concatenate([acc_ref[h] for h in range(SB_HEADS)], axis=1).astype(o_ref.dtype)


def _sb_attn(qkv, batch, seq, tq):
    n = qkv.shape[0]
    nq = seq // tq
    j = jnp.arange(tq)
    tri = (j[:, None] > j[None, :]).astype(BF16)
    return pl.pallas_call(
        functools.partial(_sb_attn_kernel, tq=tq),
        grid=(batch, nq),
        in_specs=[pl.BlockSpec((tq, SB_WIDTH), lambda b, i: (b * nq + i, 0)),
                  pl.BlockSpec((seq, SB_WIDTH), lambda b, i: (b, 1)),
                  pl.BlockSpec((seq, SB_WIDTH), lambda b, i: (b, 2)),
                  _const_spec((tq, tq))],
        out_specs=pl.BlockSpec((tq, SB_WIDTH), lambda b, i: (b * nq + i, 0)),
        out_shape=jax.ShapeDtypeStruct((n, SB_WIDTH), BF16),
        scratch_shapes=[pltpu.VMEM((SB_HEADS, tq, LANES), F32), pltpu.VMEM((SB_HEADS, tq, SB_HEAD_DIM), F32)],
        compiler_params=_params(2),
        name="sb_attn",
    )(qkv, qkv, qkv, tri)


def _merge_kernel(x_ref, gate_ref, conv_ref, halo_ref, omla_ref, osb_ref, cw_ref, wa_ref, wb_ref, wc_ref,
                  wo_ref, fn_ref, wr_cat_ref, br_ref, low_ref, xo_ref, h2_ref, route_ref, cnt_ref,
                  *, tm, sub, tiles_per_seq):
    i = pl.program_id(0)
    f = lambda r: r.astype(F32)
    cw = cw_ref[...]

    @pl.when(i == 0)
    def _():
        cnt_ref[...] = jnp.zeros_like(cnt_ref)

    counts = cnt_ref[...]
    for r0 in range(0, tm, sub):
        rows = slice(r0, r0 + sub)
        conv = conv_ref[rows, :]
        u = f(conv[:, 2 * CONV_WIDTH:]) * f(conv[:, :CONV_WIDTH])
        if r0 == 0:
            halo = halo_ref[...]
            up = f(halo[:, 2 * CONV_WIDTH:]) * f(halo[:, :CONV_WIDTH])
            up = jnp.where((i % tiles_per_seq) == 0, 0.0, up)
        else:
            halo = conv_ref[r0 - 8:r0, :]
            up = f(halo[:, 2 * CONV_WIDTH:]) * f(halo[:, :CONV_WIDTH])
        ue = jnp.concatenate([up, u], axis=0)
        y = cw[0:1, :] * ue[6:sub + 6] + cw[1:2, :] * ue[7:sub + 7] + cw[2:3, :] * u
        ya = (f(conv[:, CONV_WIDTH:2 * CONV_WIDTH]) * y).astype(BF16)
        gate = gate_ref[rows, :]
        merged = (f(gate[:, :D_MODEL]) * jnp.dot(ya, wa_ref[...], preferred_element_type=F32)
                  + f(gate[:, D_MODEL:2 * D_MODEL]) * jnp.dot(omla_ref[rows, :], wb_ref[...],
                                                              preferred_element_type=F32)
                  + f(gate[:, 2 * D_MODEL:]) * jnp.dot(osb_ref[rows, :], wc_ref[...], preferred_element_type=F32))
        x_new = x_ref[rows, :] + jnp.dot(merged.astype(BF16), wo_ref[...], preferred_element_type=F32)
        xo_ref[rows, :] = x_new
        h2 = _rms(x_new, fn_ref[...])
        h2_hi = h2.astype(BF16)
        for c in range(ROW_TILES):
            h2_ref[pl.ds(r0 * ROW_TILES + c, sub, stride=ROW_TILES), :] = (
                h2_hi[:, c * LANES:(c + 1) * LANES].astype(F32))
        h2_lo = (h2 - h2_hi.astype(F32)).astype(BF16)
        both = jnp.dot(h2_hi, wr_cat_ref[...], preferred_element_type=F32)
        logits = (both[:, :ROUTER_COLS] + both[:, ROUTER_COLS:]
                  + jnp.dot(h2_lo, wr_cat_ref[:, :ROUTER_COLS], preferred_element_type=F32)) + br_ref[...]
        w1, w2, i1, i2 = _route(logits)

        lane = lax.broadcasted_iota(jnp.int32, logits.shape, 1)
        pick1, pick2 = lane == i1, lane == i2
        onehot = jnp.where(pick1 | pick2, 1.0, 0.0)
        before = jnp.dot(low_ref[...], onehot.astype(BF16), preferred_element_type=F32) + counts
        rank1 = jnp.sum(jnp.where(pick1, before, 0.0), axis=-1, keepdims=True)
        rank2 = jnp.sum(jnp.where(pick2, before, 0.0), axis=-1, keepdims=True)
        counts = counts + jnp.sum(onehot, axis=0, keepdims=True)
        cols = (w1, w2, i1.astype(F32), i2.astype(F32), rank1, rank2)
        route = jnp.zeros(logits.shape, F32)
        for k, col in enumerate(cols):
            route = jnp.where(lane == k, col, route)
        route_ref[rows, :] = route
    cnt_ref[...] = counts


def _route(logits):
    lane = lax.broadcasted_iota(jnp.int32, logits.shape, 1)
    big = jnp.int32(1 << 30)
    neg = -jnp.inf
    is_group = (lane >= N_EXPERTS) & (lane < N_EXPERTS + N_GROUPS)
    gl = jnp.where(is_group, logits, neg)
    gmax = jnp.max(gl, axis=-1, keepdims=True)
    g_w = 1.0 / jnp.sum(jnp.exp(gl - gmax), axis=-1, keepdims=True)
    g_idx = jnp.min(jnp.where(gl == gmax, lane, big), axis=-1, keepdims=True) - N_EXPERTS
    lo = g_idx * EXPERTS_PER_GROUP
    el = jnp.where((lane >= lo) & (lane < lo + EXPERTS_PER_GROUP), logits, neg)
    m1 = jnp.max(el, axis=-1, keepdims=True)
    i1 = jnp.min(jnp.where(el == m1, lane, big), axis=-1, keepdims=True)
    el2 = jnp.where(lane == i1, neg, el)
    m2 = jnp.max(el2, axis=-1, keepdims=True)
    i2 = jnp.min(jnp.where(el2 == m2, lane, big), axis=-1, keepdims=True)
    r = jnp.exp(m2 - m1)
    w1 = g_w / (1.0 + r)
    w2 = g_w * r / (1.0 + r)
    return w1, w2, i1, i2


def _merge(x2, gates, conv, o_mla, o_sb, conv_w, wa, wb, wc, wo, ffn_norm, wr_cat, b_router, seq, tm):
    n = x2.shape[0]
    row = lambda w: pl.BlockSpec((tm, w), lambda i: (i, 0))
    halo = pl.BlockSpec((8, CONV_COLS), lambda i: (jnp.maximum(i * (tm // 8) - 1, 0), 0))
    sub = tm
    r = jnp.arange(sub)
    lower = (r[:, None] > r[None, :]).astype(BF16)
    return pl.pallas_call(
        functools.partial(_merge_kernel, tm=tm, sub=sub, tiles_per_seq=seq // tm),
        grid=(n // tm,),
        in_specs=[row(D_MODEL), row(GATE_COLS), row(CONV_COLS), halo, row(CONV_WIDTH), row(SB_WIDTH),
                  _const_spec((8, CONV_WIDTH)), _const_spec((CONV_WIDTH, D_MODEL)),
                  _const_spec((MLA_HEADS * MLA_V, D_MODEL)), _const_spec((SB_WIDTH, D_MODEL)),
                  _const_spec((D_MODEL, D_MODEL)), _const_spec((1, D_MODEL)),
                  _const_spec((D_MODEL, 2 * ROUTER_COLS)),
                  _const_spec((1, ROUTER_COLS)), _const_spec((sub, sub))],
        out_specs=[row(D_MODEL), pl.BlockSpec((tm * ROW_TILES, LANES), lambda i: (i, 0)), row(ROUTER_COLS),
                   _const_spec((1, ROUTER_COLS))],
        out_shape=[jax.ShapeDtypeStruct((n, D_MODEL), F32), jax.ShapeDtypeStruct((n * ROW_TILES, LANES), F32),
                   jax.ShapeDtypeStruct((n, ROUTER_COLS), F32), jax.ShapeDtypeStruct((1, ROUTER_COLS), F32)],
        compiler_params=_params(1),
        name="merge",
    )(x2, gates, conv, conv, o_mla, o_sb, conv_w, wa, wb, wc, wo, ffn_norm, wr_cat, b_router, lower)


def _pack_router(w_rg, b_rg, w_re, b_re):
    pad = ROUTER_COLS - N_EXPERTS - N_GROUPS
    w = jnp.concatenate([w_re, w_rg, jnp.zeros((D_MODEL, pad), F32)], axis=1)
    b = jnp.concatenate([b_re, b_rg, jnp.zeros((pad,), F32)])[None, :]
    hi = w.astype(BF16)
    lo = (w - hi.astype(F32)).astype(BF16)
    return jnp.concatenate([hi, lo], axis=1), b


def _moe_plan(route, counts, n):
    cnt = counts[0, :N_EXPERTS].astype(jnp.int32)
    padded = (cnt + MOE_TILE - 1) // MOE_TILE * MOE_TILE
    ends = jnp.cumsum(padded)
    starts = ends - padded
    n_used = ends[-1] // MOE_TILE
    expert = route[:, 2:4].astype(jnp.int32)
    rank = route[:, 4:6].astype(jnp.int32)
    chosen = expert[..., None] == jnp.arange(N_EXPERTS, dtype=jnp.int32)
    pos = jnp.sum(jnp.where(chosen, starts, 0), axis=-1) + rank
    n_tiles = (2 * n) // MOE_TILE + N_EXPERTS
    tile_start = jnp.minimum(jnp.arange(n_tiles, dtype=jnp.int32), n_used - 1) * MOE_TILE
    tile_expert = jnp.sum((ends[None, :] <= tile_start[:, None]).astype(jnp.int32), axis=1)
    last_tile = jnp.where(padded > 0, ends // MOE_TILE - 1, -1)
    tail = n_used + jnp.arange(N_EXPERTS, dtype=jnp.int32)
    zero_tiles = jnp.concatenate([last_tile, jnp.where(tail < n_tiles, tail, -1)])
    return pos, tile_expert, n_used.reshape(1), zero_tiles, n_tiles


def _dispatch_kernel(zt_ref, pos_ref, h_ref, xs_hbm, zero_ref, sem, zsem, *, td):
    t = pl.program_id(0)

    def fill(j):
        return pltpu.make_async_copy(zero_ref, _token_rows(xs_hbm, zt_ref[j] * MOE_TILE, MOE_TILE), zsem)

    @pl.when(t == 0)
    def _():
        zero_ref[...] = jnp.zeros_like(zero_ref)
        for j in range(2 * N_EXPERTS):
            @pl.when(zt_ref[j] >= 0)
            def _():
                fill(j).start()
        for j in range(2 * N_EXPERTS):
            @pl.when(zt_ref[j] >= 0)
            def _():
                fill(j).wait()

    for r in range(td):
        src = h_ref.at[pl.ds(r * ROW_TILES, ROW_TILES)]
        for k in range(2):
            pltpu.make_async_copy(src, _token_rows(xs_hbm, pos_ref[0, 0, 2 * r + k]), sem).start(priority=k)

    for k in range(2):
        pltpu.make_async_copy(h_ref, _token_rows(xs_hbm, 0, td), sem).wait()


def _dispatch(h2, pos3, zero_tiles, n_tiles, td):
    n = h2.shape[0] // ROW_TILES
    n_steps = n // td
    return pl.pallas_call(
        functools.partial(_dispatch_kernel, td=td),
        grid_spec=pltpu.PrefetchScalarGridSpec(
            num_scalar_prefetch=1, grid=(n_steps,),
            in_specs=[pl.BlockSpec((1, 1, 2 * td), lambda t, zt: (t, 0, 0), memory_space=pltpu.SMEM),
                      pl.BlockSpec((td * ROW_TILES, LANES), lambda t, zt: (t, 0))],
            out_specs=pl.BlockSpec(memory_space=pl.ANY),
            scratch_shapes=[pltpu.VMEM((MOE_TILE * ROW_TILES, LANES), F32), pltpu.SemaphoreType.DMA,
                            pltpu.SemaphoreType.DMA]),
        out_shape=jax.ShapeDtypeStruct((n_tiles * MOE_TILE * ROW_TILES, LANES), F32),
        compiler_params=_params(1),
        name="moe_dispatch",
    )(zero_tiles, pos3, h2)


def _experts_kernel(te_ref, nu_ref, xs_ref, wg_ref, wu_ref, wd_ref, y_ref, wg_bf, wu_bf, wd_bf):
    t = pl.program_id(0)

    @pl.when(t >= nu_ref[0])
    def _():
        y_ref[...] = jnp.zeros_like(y_ref)

    @pl.when((t == 0) | (te_ref[t] != te_ref[jnp.maximum(t - 1, 0)]))
    def _():
        wg_bf[...] = wg_ref[0].astype(BF16)
        wu_bf[...] = wu_ref[0].astype(BF16)
        wd_bf[...] = wd_ref[0].astype(BF16)

    @pl.when(t < nu_ref[0])
    def _():
        x = jnp.concatenate([xs_ref[pl.ds(c, MOE_TILE, stride=ROW_TILES), :] for c in range(ROW_TILES)],
                            axis=1).astype(BF16)
        g = jnp.dot(x, wg_bf[...], preferred_element_type=F32)
        u = jnp.dot(x, wu_bf[...], preferred_element_type=F32)
        act = (g * jax.nn.sigmoid(g) * u).astype(BF16)
        y = jnp.dot(act, wd_bf[...], preferred_element_type=F32)
        for c in range(ROW_TILES):
            y_ref[pl.ds(c, MOE_TILE, stride=ROW_TILES), :] = y[:, c * LANES:(c + 1) * LANES]


def _experts(xs, tile_expert, n_used, wg, wu, wd, layer, n_tiles):
    shape = (MOE_TILE * ROW_TILES, LANES)
    rows_in = pl.BlockSpec(shape, lambda t, te, nu: (jnp.minimum(t, nu[0] - 1), 0))
    rows_out = pl.BlockSpec(shape, lambda t, te, nu: (t, 0))
    weight = lambda shape: pl.BlockSpec((1,) + shape, lambda t, te, nu: (layer * N_EXPERTS + te[t], 0, 0))
    up, down = (D_MODEL, D_EXPERT), (D_EXPERT, D_MODEL)
    return pl.pallas_call(
        _experts_kernel,
        grid_spec=pltpu.PrefetchScalarGridSpec(
            num_scalar_prefetch=2, grid=(n_tiles,),
            in_specs=[rows_in, weight(up), weight(up), weight(down)],
            out_specs=rows_out,
            scratch_shapes=[pltpu.VMEM(up, BF16), pltpu.VMEM(up, BF16), pltpu.VMEM(down, BF16)]),
        out_shape=jax.ShapeDtypeStruct(xs.shape, F32),
        compiler_params=_params(1),
        name="moe_experts",
    )(tile_expert, n_used, xs, wg, wu, wd)


def _combine_norm_kernel(pos_ref, posn_ref, x_ref, route_ref, y_hbm, fin_ref, o_ref, ybuf, sem, *, tc, n_steps):
    def consume(x, issue_part):
        o_ref[...] = _rms(x, fin_ref[...])

    _combine_steps(pos_ref, posn_ref, x_ref, route_ref, y_hbm, ybuf, sem, tc=tc, n_steps=n_steps, consume=consume)


def _combine_norm(x_mid, route, y, pos, final_norm, tc):
    n = x_mid.shape[0]
    n_steps = n // tc
    in_specs, scratch = _combine_specs(tc, n_steps)
    pos3 = pos.reshape(n_steps, 1, 2 * tc)
    return pl.pallas_call(
        functools.partial(_combine_norm_kernel, tc=tc, n_steps=n_steps),
        grid=(n_steps,),
        in_specs=in_specs + [_const_spec((1, D_MODEL))],
        out_specs=pl.BlockSpec((tc, D_MODEL), lambda t: (t, 0)),
        out_shape=jax.ShapeDtypeStruct((n, D_MODEL), F32),
        scratch_shapes=scratch,
        compiler_params=_params(1),
        name="moe_combine_norm",
    )(pos3, pos3, x_mid, route, y, final_norm)


def _moe_experts(h2, route, counts, wg, wu, wd, layer):
    n = route.shape[0]
    pos, tile_expert, n_used, zero_tiles, n_tiles = _moe_plan(route, counts, n)
    td = _tile(n, 512)
    xs = _dispatch(h2, pos.reshape(n // td, 1, 2 * td), zero_tiles, n_tiles, td)
    return _experts(xs, tile_expert, n_used, wg, wu, wd, layer, n_tiles), pos


def _tile(n, pref):
    t = min(n, pref)
    assert n % t == 0, (n, t)
    return t


def kernel(x, positions, attn_norm, w_in, b_gate, conv_w, w_out_conv, q_norm, kv_norm, w_uq, w_ukv, w_out_mla, w_out_sb, w_o, ffn_norm, w_router_group, b_router_group, w_router_expert, b_router_expert, w_exp_gate, w_exp_up, w_exp_down, final_norm):
    batch, seq, d = x.shape
    assert d == D_MODEL
    n = batch * seq
    depth = w_in.shape[0]
    tm_rows = _tile(seq, 512)
    tq = _tile(seq, 256)
    x2 = x.reshape(n, d)
    cos, sin = _rope_tables(positions, _tile(n, 1024))
    wg_all = w_exp_gate.reshape(depth * N_EXPERTS, D_MODEL, D_EXPERT)
    wu_all = w_exp_up.reshape(depth * N_EXPERTS, D_MODEL, D_EXPERT)
    wd_all = w_exp_down.reshape(depth * N_EXPERTS, D_EXPERT, D_MODEL)
    w_in_packed = _pack_w_in(w_in)
    moe = None
    for l in range(depth):
        proj_args = (attn_norm[l][None, :], w_in_packed, l, b_gate[l][None, :], tm_rows)
        if moe is None:
            gates, conv, lat, sb = _in_proj(x2, *proj_args)
        else:
            x2, gates, conv, lat, sb = _combine_in_proj(*moe, *proj_args)
        wq_main, wq_swap, wk, wv = _pack_mla_weights(w_uq[l], w_ukv[l])
        q, k, v = _mla_prep(lat, cos, sin, q_norm[l][None, :], kv_norm[l][None, :], wq_main, wq_swap, wk, wv,
                            tm_rows)
        o_mla = _mla_attn(q, k, v, batch, seq, tq)
        o_sb = _sb_attn(sb, batch, seq, tq)
        wr_cat, b_router = _pack_router(w_router_group[l], b_router_group[l], w_router_expert[l],
                                        b_router_expert[l])
        conv_w8 = jnp.concatenate([conv_w[l], jnp.zeros((8 - CONV_K, CONV_WIDTH), F32)], axis=0)
        x_mid, h2, route, counts = _merge(x2, gates, conv, o_mla, o_sb, conv_w8, w_out_conv[l].astype(BF16),
                                          w_out_mla[l].astype(BF16), w_out_sb[l].astype(BF16),
                                          w_o[l].astype(BF16), ffn_norm[l][None, :], wr_cat, b_router, seq,
                                          tm_rows)
        y, pos = _moe_experts(h2, route, counts, wg_all, wu_all, wd_all, l)
        moe = (x_mid, route, y, pos)
    out = _combine_norm(*moe, final_norm[None, :], _tile(n, 256))
    return out.reshape(batch, seq, d)
```

```python
import functools

import jax
import jax.numpy as jnp
from jax import lax
from jax.experimental import pallas as pl
from jax.experimental.pallas import tpu as pltpu

D_MODEL = 1024
CONV_WIDTH = 512
CONV_K = 3
MLA_HEADS = 8
MLA_NOPE = 64
MLA_ROPE = 32
MLA_V = 64
MLA_Q_RANK = 256
MLA_KV_RANK = 128
ROPE_THETA = 10000.0
SB_HEADS = 8
SB_HEAD_DIM = 64
SB_WIDTH = SB_HEADS * SB_HEAD_DIM
N_BRANCHES = 3
OFF_CONV = 0
OFF_CQ = OFF_CONV + 3 * CONV_WIDTH
OFF_CKV = OFF_CQ + MLA_Q_RANK
OFF_KR = OFF_CKV + MLA_KV_RANK
OFF_SB = OFF_KR + MLA_ROPE
OFF_GATE = OFF_SB + 3 * SB_WIDTH
N_GROUPS = 4
EXPERTS_PER_GROUP = 8
N_EXPERTS = N_GROUPS * EXPERTS_PER_GROUP
D_EXPERT = 256
EPS = 1e-6
LOG2E = 1.4426950408889634
SB_DEAD = 160.0

LANES = 128
HEAD_PAD = 128
ROPE_HALF = MLA_ROPE // 2
GATE_COLS = N_BRANCHES * D_MODEL
CONV_COLS = 3 * CONV_WIDTH
LAT_COLS = MLA_Q_RANK + MLA_KV_RANK + 2 * HEAD_PAD
SB_COLS = 3 * SB_WIDTH
ROUTER_COLS = LANES
ROW_TILES = D_MODEL // LANES
MOE_TILE = 256
VMEM_LIMIT = 56 * 1024 * 1024

BF16 = jnp.bfloat16
F32 = jnp.float32


def _params(n_axes, vmem=VMEM_LIMIT):
    return pltpu.CompilerParams(dimension_semantics=("arbitrary",) * n_axes, vmem_limit_bytes=vmem)


def _rms(xf, gain):
    return xf * lax.rsqrt(jnp.mean(xf * xf, axis=-1, keepdims=True) + EPS) * gain


def _const_spec(shape):
    return pl.BlockSpec(shape, lambda *_: (0,) * len(shape))


def _rope_table_kernel(pos_ref, freq_ref, cos_ref, sin_ref):
    ang = pos_ref[...].astype(F32) * freq_ref[...]
    cos_ref[...] = jnp.cos(ang)
    sin_ref[...] = jnp.sin(ang)


def _rope_tables(positions, tm):
    n = positions.size
    half = ROPE_HALF
    freqs = ROPE_THETA ** (-jnp.arange(half, dtype=F32) / half)
    zeros = jnp.zeros((MLA_NOPE,), F32)
    freq_row = jnp.concatenate([zeros, freqs, freqs, jnp.zeros((HEAD_PAD - MLA_NOPE - MLA_ROPE,), F32)])[None, :]
    pos = positions.reshape(n, 1)
    return pl.pallas_call(
        _rope_table_kernel,
        grid=(n // tm,),
        in_specs=[pl.BlockSpec((tm, 1), lambda i: (i, 0)), _const_spec((1, HEAD_PAD))],
        out_specs=[pl.BlockSpec((tm, HEAD_PAD), lambda i: (i, 0))] * 2,
        out_shape=[jax.ShapeDtypeStruct((n, HEAD_PAD), F32)] * 2,
        compiler_params=_params(1),
        name="rope_tables",
    )(pos, freq_row)


def _token_rows(ref, first_token, count=1):
    start = pl.multiple_of(first_token * ROW_TILES, ROW_TILES)
    return ref.at[pl.ds(start, count * ROW_TILES)]


def _combine_steps(pos_ref, posn_ref, x_ref, route_ref, y_hbm, ybuf, sem, *, tc, n_steps, consume,
                   spread_issue=False):
    t = pl.program_id(0)

    def issue(p_ref, slot, part=0, parts=1):
        for r in range(2 * tc * part // parts, 2 * tc * (part + 1) // parts):
            pltpu.make_async_copy(_token_rows(y_hbm, p_ref[0, 0, r]),
                                  ybuf.at[slot, pl.ds(r * ROW_TILES, ROW_TILES)],
                                  sem.at[slot]).start(priority=0 if spread_issue else r % 2)

    def wait(slot):
        pltpu.make_async_copy(_token_rows(y_hbm, 0, 2 * tc), ybuf.at[slot], sem.at[slot]).wait()

    @pl.when(t == 0)
    def _():
        issue(pos_ref, 0)

    def run(slot):
        if not spread_issue:
            issue(posn_ref, 1 - slot)
        wait(slot)
        w1, w2 = route_ref[:, 0:1], route_ref[:, 1:2]
        outs = []
        for c in range(ROW_TILES):
            y1 = ybuf[slot, pl.ds(c, tc, stride=2 * ROW_TILES), :]
            y2 = ybuf[slot, pl.ds(ROW_TILES + c, tc, stride=2 * ROW_TILES), :]
            outs.append(x_ref[:, c * LANES:(c + 1) * LANES] + w1 * y1 + w2 * y2)
        issue_part = functools.partial(issue, posn_ref, 1 - slot) if spread_issue else None
        consume(jnp.concatenate(outs, axis=1), issue_part)

    for slot in range(2):
        @pl.when(t % 2 == slot)
        def _():
            run(slot)

    @pl.when(t == n_steps - 1)
    def _():
        wait(n_steps % 2)


def _combine_specs(tc, n_steps):
    pos_spec = lambda step: pl.BlockSpec((1, 1, 2 * tc), lambda t: (step(t), 0, 0), memory_space=pltpu.SMEM)
    row = lambda w: pl.BlockSpec((tc, w), lambda t: (t, 0))
    in_specs = [pos_spec(lambda t: t), pos_spec(lambda t: jnp.minimum(t + 1, n_steps - 1)), row(D_MODEL),
                row(ROUTER_COLS), pl.BlockSpec(memory_space=pl.ANY)]
    scratch = [pltpu.VMEM((2, 2 * tc * ROW_TILES, LANES), F32), pltpu.SemaphoreType.DMA((2,))]
    return in_specs, scratch


def _project(x, g_ref, w_ref, b_ref, gate_ref, conv_ref, lat_ref, sb_ref, chunk, before_chunk=None):
    h = _rms(x, g_ref[...]).astype(BF16)
    n_chunks = sum(-(-w // chunk) for w in PROJ_WIDTHS)
    done = [0]

    def run(out_ref, col0, width, epilogue):
        for c in range(0, width, chunk):
            cw = min(chunk, width - c)
            if before_chunk is not None:
                before_chunk(done[0], n_chunks)
            done[0] += 1
            acc = jnp.dot(h, w_ref[:, col0 + c:col0 + c + cw], preferred_element_type=F32)
            out_ref[:, c:c + cw] = epilogue(acc, c, cw).astype(out_ref.dtype)

    run(gate_ref, 0, GATE_COLS, lambda a, c, cw: jax.nn.sigmoid(a + b_ref[:, c:c + cw]))
    ident = lambda a, c, cw: a
    run(conv_ref, GATE_COLS, CONV_COLS, ident)
    run(lat_ref, GATE_COLS + CONV_COLS, LAT_COLS, ident)
    run(sb_ref, GATE_COLS + CONV_COLS + LAT_COLS, SB_COLS, ident)


PROJ_WIDTHS = (GATE_COLS, CONV_COLS, LAT_COLS, SB_COLS)
PROJ_CHUNK = 512


def _in_proj_kernel(x_ref, g_ref, w_ref, b_ref, gate_ref, conv_ref, lat_ref, sb_ref):
    _project(x_ref[...], g_ref, w_ref, b_ref, gate_ref, conv_ref, lat_ref, sb_ref, PROJ_CHUNK)


def _layer_spec(stacked, layer, **kwargs):
    shape = stacked.shape[1:]
    return pl.BlockSpec((None,) + shape, lambda *_: (layer,) + (0,) * len(shape), **kwargs)


def _in_proj(x2, gain, w_packed, layer, b_gate, tm):
    n = x2.shape[0]
    return pl.pallas_call(
        _in_proj_kernel,
        grid=(n // tm,),
        in_specs=[pl.BlockSpec((tm, D_MODEL), lambda i: (i, 0)), _const_spec((1, D_MODEL)),
                  _layer_spec(w_packed, layer), _const_spec((1, GATE_COLS))],
        out_specs=[pl.BlockSpec((tm, w), lambda i: (i, 0)) for w in PROJ_WIDTHS],
        out_shape=[jax.ShapeDtypeStruct((n, w), BF16) for w in PROJ_WIDTHS],
        compiler_params=_params(1),
        name="in_proj",
    )(x2, gain, w_packed, b_gate)


def _combine_in_proj_kernel(pos_ref, posn_ref, x_ref, route_ref, y_hbm, g_ref, w_ref, b_ref,
                            xo_ref, gate_ref, conv_ref, lat_ref, sb_ref, ybuf, sem, *, tm, n_steps):
    def consume(x, issue_part):
        xo_ref[...] = x
        _project(x, g_ref, w_ref, b_ref, gate_ref, conv_ref, lat_ref, sb_ref, PROJ_CHUNK, before_chunk=issue_part)

    _combine_steps(pos_ref, posn_ref, x_ref, route_ref, y_hbm, ybuf, sem, tc=tm, n_steps=n_steps, consume=consume,
                   spread_issue=True)


def _combine_in_proj(x_mid, route, y, pos, gain, w_packed, layer, b_gate, tm):
    n = x_mid.shape[0]
    n_steps = n // tm
    in_specs, scratch = _combine_specs(tm, n_steps)
    pos3 = pos.reshape(n_steps, 1, 2 * tm)
    widths = (D_MODEL,) + PROJ_WIDTHS
    return pl.pallas_call(
        functools.partial(_combine_in_proj_kernel, tm=tm, n_steps=n_steps),
        grid=(n_steps,),
        in_specs=in_specs + [_const_spec((1, D_MODEL)),
                             _layer_spec(w_packed, layer, pipeline_mode=pl.Buffered(1)),
                             _const_spec((1, GATE_COLS))],
        out_specs=[pl.BlockSpec((tm, w), lambda t: (t, 0)) for w in widths],
        out_shape=[jax.ShapeDtypeStruct((n, D_MODEL), F32)] + [jax.ShapeDtypeStruct((n, w), BF16)
                                                               for w in PROJ_WIDTHS],
        scratch_shapes=scratch,
        compiler_params=_params(1),
        name="combine_in_proj",
    )(pos3, pos3, x_mid, route, y, gain, w_packed, b_gate)


def _pack_w_in_kernel(w_ref, o_ref):
    rows = w_ref.shape[0]
    col = 0

    def put(values):
        nonlocal col
        o_ref[:, col:col + values.shape[1]] = values.astype(BF16)
        col += values.shape[1]

    put(w_ref[:, OFF_GATE:])
    put(w_ref[:, OFF_CONV:OFF_CQ])
    put(w_ref[:, OFF_CQ:OFF_KR])
    kr = w_ref[:, OFF_KR:OFF_SB]
    x1, x2 = kr[:, :ROPE_HALF], kr[:, ROPE_HALF:]
    z_lo = jnp.zeros((rows, MLA_NOPE), F32)
    z_hi = jnp.zeros((rows, HEAD_PAD - MLA_NOPE - MLA_ROPE), F32)
    put(jnp.concatenate([z_lo, x1, x2, z_hi], axis=1))
    put(jnp.concatenate([z_lo, -x2, x1, z_hi], axis=1))
    put(w_ref[:, OFF_SB:OFF_SB + SB_WIDTH] * (LOG2E * SB_HEAD_DIM ** -0.5))
    put(w_ref[:, OFF_SB + SB_WIDTH:OFF_GATE])
    assert col == o_ref.shape[1]


def _pack_w_in(w_in, rows=128):
    depth, d, cols = w_in.shape
    total = sum(PROJ_WIDTHS)
    return pl.pallas_call(
        _pack_w_in_kernel,
        grid=(depth, d // rows),
        in_specs=[pl.BlockSpec((None, rows, cols), lambda l, i: (l, i, 0))],
        out_specs=pl.BlockSpec((None, rows, total), lambda l, i: (l, i, 0)),
        out_shape=jax.ShapeDtypeStruct((depth, d, total), BF16),
        compiler_params=_params(2),
        name="pack_w_in",
    )(w_in)


def _mla_prep_kernel(lat_ref, cos_ref, sin_ref, qn_ref, kvn_ref, wqm_ref, wqs_ref, wk_ref, wv_ref,
                     q_ref, k_ref, v_ref, *, scale):
    cos, sin = cos_ref[...], sin_ref[...]
    cq = _rms(lat_ref[:, :MLA_Q_RANK].astype(F32), qn_ref[...]).astype(BF16)
    ckv = _rms(lat_ref[:, MLA_Q_RANK:MLA_Q_RANK + MLA_KV_RANK].astype(F32), kvn_ref[...]).astype(BF16)
    kr0 = MLA_Q_RANK + MLA_KV_RANK
    kr = (lat_ref[:, kr0:kr0 + HEAD_PAD].astype(F32) * cos
          + lat_ref[:, kr0 + HEAD_PAD:kr0 + 2 * HEAD_PAD].astype(F32) * sin)
    qm = jnp.dot(cq, wqm_ref[...], preferred_element_type=F32)
    qs = jnp.dot(cq, wqs_ref[...], preferred_element_type=F32)
    kn = jnp.dot(ckv, wk_ref[...], preferred_element_type=F32)
    for h in range(MLA_HEADS):
        sl = slice(h * HEAD_PAD, (h + 1) * HEAD_PAD)
        q_ref[:, sl] = ((qm[:, sl] * cos + qs[:, sl] * sin) * scale).astype(BF16)
        k_ref[:, sl] = (kn[:, sl] + kr).astype(BF16)
    v = jnp.dot(ckv, wv_ref[...], preferred_element_type=F32)
    lane = lax.broadcasted_iota(jnp.int32, v.shape, 1)
    v_ref[...] = jnp.where(lane % HEAD_PAD == MLA_V, 1.0, v).astype(BF16)


def _mla_prep(lat, cos, sin, q_norm, kv_norm, wq_main, wq_swap, wk, wv, tm):
    n = lat.shape[0]
    hw = MLA_HEADS * HEAD_PAD
    row = lambda w: pl.BlockSpec((tm, w), lambda i: (i, 0))
    return pl.pallas_call(
        functools.partial(_mla_prep_kernel, scale=LOG2E * (MLA_NOPE + MLA_ROPE) ** -0.5),
        grid=(n // tm,),
        in_specs=[row(LAT_COLS), row(HEAD_PAD), row(HEAD_PAD), _const_spec((1, MLA_Q_RANK)),
                  _const_spec((1, MLA_KV_RANK)), _const_spec((MLA_Q_RANK, hw)), _const_spec((MLA_Q_RANK, hw)),
                  _const_spec((MLA_KV_RANK, hw)), _const_spec((MLA_KV_RANK, hw))],
        out_specs=[row(hw), row(hw), row(hw)],
        out_shape=[jax.ShapeDtypeStruct((n, hw), BF16)] * 3,
        compiler_params=_params(1),
        name="mla_prep",
    )(lat, cos, sin, q_norm, kv_norm, wq_main, wq_swap, wk, wv)


def _pack_mla_weights(w_uq, w_ukv):
    qd = MLA_NOPE + MLA_ROPE
    z_hi = jnp.zeros((MLA_Q_RANK, HEAD_PAD - qd), w_uq.dtype)
    z_lo = jnp.zeros((MLA_Q_RANK, MLA_NOPE), w_uq.dtype)
    main, swap, wk, wv = [], [], [], []
    for h in range(MLA_HEADS):
        wq = w_uq[:, h * qd:(h + 1) * qd]
        nope, x1, x2 = wq[:, :MLA_NOPE], wq[:, MLA_NOPE:MLA_NOPE + ROPE_HALF], wq[:, MLA_NOPE + ROPE_HALF:]
        main += [nope, x1, x2, z_hi]
        swap += [z_lo, -x2, x1, z_hi]
        kv = w_ukv[:, h * (MLA_NOPE + MLA_V):(h + 1) * (MLA_NOPE + MLA_V)]
        wk += [kv[:, :MLA_NOPE], jnp.zeros((MLA_KV_RANK, HEAD_PAD - MLA_NOPE), w_ukv.dtype)]
        wv += [kv[:, MLA_NOPE:], jnp.zeros((MLA_KV_RANK, HEAD_PAD - MLA_V), w_ukv.dtype)]
    cat = lambda parts: jnp.concatenate(parts, axis=1).astype(BF16)
    return cat(main), cat(swap), cat(wk), cat(wv)


def _mla_attn_kernel(q_ref, k_ref, v_ref, o_ref, m_ref, acc_ref, *, tq):
    i = pl.program_id(1)
    row = lax.broadcasted_iota(jnp.int32, (tq, tq), 0)
    col = lax.broadcasted_iota(jnp.int32, (tq, tq), 1)
    causal = row >= col
    dn = (((1,), (1,)), ((), ()))

    def step(j, width, diagonal=False):
        start = pl.multiple_of(j * tq, tq)
        for h in range(MLA_HEADS):
            hs = slice(h * HEAD_PAD, (h + 1) * HEAD_PAD)
            s = lax.dot_general(q_ref[:, hs], k_ref[pl.ds(start, width), hs], dn, preferred_element_type=F32)
            if diagonal:
                s = jnp.where(causal, s, -jnp.inf)
            m_cur = jnp.max(s, axis=1, keepdims=True)
            if diagonal:
                m_new = jnp.broadcast_to(m_cur, (tq, LANES))
            else:
                m_old = m_ref[h]
                m_new = jnp.maximum(m_old, m_cur)
            p = jnp.exp2(s - jnp.concatenate([m_new] * (width // LANES), axis=1))
            pv = jnp.dot(p.astype(BF16), v_ref[pl.ds(start, width), hs], preferred_element_type=F32)
            if diagonal:
                acc_ref[h] = pv
            else:
                acc_ref[h] = jnp.exp2(m_old - m_new) * acc_ref[h] + pv
            m_ref[h] = m_new

    @pl.when(i % 2 == 0)
    def _():
        step(i, tq, diagonal=True)

    @pl.when(i % 2 == 1)
    def _():
        step(i, tq, diagonal=True)
        step(i - 1, tq)

    unroll = 4
    even = i - i % 2

    def body(jj, carry):
        for u in range(unroll):
            step(unroll * jj + u, tq)
        return carry

    lax.fori_loop(0, even // unroll, body, 0)

    @pl.when(even % unroll == 2)
    def _():
        step(even - 2, tq)
        step(even - 1, tq)

    outs = []
    for h in range(MLA_HEADS):
        acc = acc_ref[h]
        outs.append(acc[:, :MLA_V] / acc[:, MLA_V:MLA_V + 1])
    o_ref[...] = jnp.concatenate(outs, axis=1).astype(o_ref.dtype)


def _mla_attn(q, k, v, batch, seq, tq):
    n = q.shape[0]
    hw = MLA_HEADS * HEAD_PAD
    vw = MLA_HEADS * MLA_V
    nq = seq // tq
    return pl.pallas_call(
        functools.partial(_mla_attn_kernel, tq=tq),
        grid=(batch, nq),
        in_specs=[pl.BlockSpec((tq, hw), lambda b, i: (b * nq + i, 0)),
                  pl.BlockSpec((seq, hw), lambda b, i: (b, 0)),
                  pl.BlockSpec((seq, hw), lambda b, i: (b, 0))],
        out_specs=pl.BlockSpec((tq, vw), lambda b, i: (b * nq + i, 0)),
        out_shape=jax.ShapeDtypeStruct((n, vw), BF16),
        scratch_shapes=[pltpu.VMEM((MLA_HEADS, tq, LANES), F32), pltpu.VMEM((MLA_HEADS, tq, HEAD_PAD), F32)],
        compiler_params=_params(2),
        name="mla_attn",
    )(q, k, v)


def _sb_attn_kernel(q_ref, k_ref, v_ref, tri_ref, o_ref, run_ref, acc_ref, *, tq):
    i = pl.program_id(1)
    row = lax.broadcasted_iota(jnp.int32, (tq, tq), 0)
    col = lax.broadcasted_iota(jnp.int32, (tq, tq), 1)
    strict = row > col
    dn = (((1,), (1,)), ((), ()))
    tri = tri_ref[...]

    def step(j, diagonal):
        start = pl.multiple_of(j * tq, tq)
        for h in range(SB_HEADS):
            hs = slice(h * SB_HEAD_DIM, (h + 1) * SB_HEAD_DIM)
            z = lax.dot_general(q_ref[:, hs], k_ref[pl.ds(start, tq), hs], dn, preferred_element_type=F32)
            sp = jnp.maximum(z, 0.0) + jnp.log(1.0 + jnp.exp2(-jnp.abs(z))) * LOG2E
            if diagonal:
                sp = jnp.where(strict, sp, 0.0)
            later = jnp.dot(sp.astype(BF16), tri, preferred_element_type=F32)
            a = jnp.exp2(z - sp - later)
            if diagonal:
                a = jnp.where(strict, a, 0.0)
            av = jnp.dot(a.astype(BF16), v_ref[pl.ds(start, tq), hs], preferred_element_type=F32)
            total = jnp.sum(sp, axis=1, keepdims=True)
            if diagonal:
                acc_ref[h] = av
                run_ref[h] = jnp.broadcast_to(total, (tq, LANES))
            else:
                run = run_ref[h]
                acc_ref[h] += jnp.exp2(-run[:, :SB_HEAD_DIM]) * av
                run_ref[h] = run + total

    @pl.when(i == 0)
    def _():
        step(0, True)

    @pl.when(i > 0)
    def _():
        step(i, True)
        step(i - 1, False)

    def alive():
        return (jnp.min(run_ref[...]) < SB_DEAD).astype(jnp.int32)

    def cond(carry):
        t, live = carry
        return (t < i) & (live > 0)

    def body(carry):
        t, _ = carry
        step(i - 1 - t, False)
        return t + 1, alive()

    lax.while_loop(cond, body, (jnp.int32(1), alive()))
    o_ref[...] = jnp.concatenate([acc_ref[h] for h in range(SB_HEADS)], axis=1).astype(o_ref.dtype)


def _sb_attn(qkv, batch, seq, tq):
    n = qkv.shape[0]
    nq = seq // tq
    j = jnp.arange(tq)
    tri = (j[:, None] > j[None, :]).astype(BF16)
    return pl.pallas_call(
        functools.partial(_sb_attn_kernel, tq=tq),
        grid=(batch, nq),
        in_specs=[pl.BlockSpec((tq, SB_WIDTH), lambda b, i: (b * nq + i, 0)),
                  pl.BlockSpec((seq, SB_WIDTH), lambda b, i: (b, 1)),
                  pl.BlockSpec((seq, SB_WIDTH), lambda b, i: (b, 2)),
                  _const_spec((tq, tq))],
        out_specs=pl.BlockSpec((tq, SB_WIDTH), lambda b, i: (b * nq + i, 0)),
        out_shape=jax.ShapeDtypeStruct((n, SB_WIDTH), BF16),
        scratch_shapes=[pltpu.VMEM((SB_HEADS, tq, LANES), F32), pltpu.VMEM((SB_HEADS, tq, SB_HEAD_DIM), F32)],
        compiler_params=_params(2),
        name="sb_attn",
    )(qkv, qkv, qkv, tri)


def _merge_kernel(x_ref, gate_ref, conv_ref, halo_ref, omla_ref, osb_ref, cw_ref, wa_ref, wb_ref, wc_ref,
                  wo_ref, fn_ref, wr_cat_ref, br_ref, low_ref, xo_ref, h2_ref, route_ref, cnt_ref,
                  *, tm, sub, tiles_per_seq):
    i = pl.program_id(0)
    f = lambda r: r.astype(F32)
    cw = cw_ref[...]

    @pl.when(i == 0)
    def _():
        cnt_ref[...] = jnp.zeros_like(cnt_ref)

    counts = cnt_ref[...]
    for r0 in range(0, tm, sub):
        rows = slice(r0, r0 + sub)
        conv = conv_ref[rows, :]
        u = f(conv[:, 2 * CONV_WIDTH:]) * f(conv[:, :CONV_WIDTH])
        if r0 == 0:
            halo = halo_ref[...]
            up = f(halo[:, 2 * CONV_WIDTH:]) * f(halo[:, :CONV_WIDTH])
            up = jnp.where((i % tiles_per_seq) == 0, 0.0, up)
        else:
            halo = conv_ref[r0 - 8:r0, :]
            up = f(halo[:, 2 * CONV_WIDTH:]) * f(halo[:, :CONV_WIDTH])
        ue = jnp.concatenate([up, u], axis=0)
        y = cw[0:1, :] * ue[6:sub + 6] + cw[1:2, :] * ue[7:sub + 7] + cw[2:3, :] * u
        ya = (f(conv[:, CONV_WIDTH:2 * CONV_WIDTH]) * y).astype(BF16)
        gate = gate_ref[rows, :]
        merged = (f(gate[:, :D_MODEL]) * jnp.dot(ya, wa_ref[...], preferred_element_type=F32)
                  + f(gate[:, D_MODEL:2 * D_MODEL]) * jnp.dot(omla_ref[rows, :], wb_ref[...],
                                                              preferred_element_type=F32)
                  + f(gate[:, 2 * D_MODEL:]) * jnp.dot(osb_ref[rows, :], wc_ref[...], preferred_element_type=F32))
        x_new = x_ref[rows, :] + jnp.dot(merged.astype(BF16), wo_ref[...], preferred_element_type=F32)
        xo_ref[rows, :] = x_new
        h2 = _rms(x_new, fn_ref[...])
        h2_hi = h2.astype(BF16)
        for c in range(ROW_TILES):
            h2_ref[pl.ds(r0 * ROW_TILES + c, sub, stride=ROW_TILES), :] = (
                h2_hi[:, c * LANES:(c + 1) * LANES].astype(F32))
        h2_lo = (h2 - h2_hi.astype(F32)).astype(BF16)
        both = jnp.dot(h2_hi, wr_cat_ref[...], preferred_element_type=F32)
        logits = (both[:, :ROUTER_COLS] + both[:, ROUTER_COLS:]
                  + jnp.dot(h2_lo, wr_cat_ref[:, :ROUTER_COLS], preferred_element_type=F32)) + br_ref[...]
        w1, w2, i1, i2 = _route(logits)

        lane = lax.broadcasted_iota(jnp.int32, logits.shape, 1)
        pick1, pick2 = lane == i1, lane == i2
        onehot = jnp.where(pick1 | pick2, 1.0, 0.0)
        before = jnp.dot(low_ref[...], onehot.astype(BF16), preferred_element_type=F32) + counts
        rank1 = jnp.sum(jnp.where(pick1, before, 0.0), axis=-1, keepdims=True)
        rank2 = jnp.sum(jnp.where(pick2, before, 0.0), axis=-1, keepdims=True)
        counts = counts + jnp.sum(onehot, axis=0, keepdims=True)
        cols = (w1, w2, i1.astype(F32), i2.astype(F32), rank1, rank2)
        route = jnp.zeros(logits.shape, F32)
        for k, col in enumerate(cols):
            route = jnp.where(lane == k, col, route)
        route_ref[rows, :] = route
    cnt_ref[...] = counts


def _route(logits):
    lane = lax.broadcasted_iota(jnp.int32, logits.shape, 1)
    big = jnp.int32(1 << 30)
    neg = -jnp.inf
    is_group = (lane >= N_EXPERTS) & (lane < N_EXPERTS + N_GROUPS)
    gl = jnp.where(is_group, logits, neg)
    gmax = jnp.max(gl, axis=-1, keepdims=True)
    g_w = 1.0 / jnp.sum(jnp.exp(gl - gmax), axis=-1, keepdims=True)
    g_idx = jnp.min(jnp.where(gl == gmax, lane, big), axis=-1, keepdims=True) - N_EXPERTS
    lo = g_idx * EXPERTS_PER_GROUP
    el = jnp.where((lane >= lo) & (lane < lo + EXPERTS_PER_GROUP), logits, neg)
    m1 = jnp.max(el, axis=-1, keepdims=True)
    i1 = jnp.min(jnp.where(el == m1, lane, big), axis=-1, keepdims=True)
    el2 = jnp.where(lane == i1, neg, el)
    m2 = jnp.max(el2, axis=-1, keepdims=True)
    i2 = jnp.min(jnp.where(el2 == m2, lane, big), axis=-1, keepdims=True)
    r = jnp.exp(m2 - m1)
    w1 = g_w / (1.0 + r)
    w2 = g_w * r / (1.0 + r)
    return w1, w2, i1, i2


def _merge(x2, gates, conv, o_mla, o_sb, conv_w, wa, wb, wc, wo, ffn_norm, wr_cat, b_router, seq, tm):
    n = x2.shape[0]
    row = lambda w: pl.BlockSpec((tm, w), lambda i: (i, 0))
    halo = pl.BlockSpec((8, CONV_COLS), lambda i: (jnp.maximum(i * (tm // 8) - 1, 0), 0))
    sub = tm
    r = jnp.arange(sub)
    lower = (r[:, None] > r[None, :]).astype(BF16)
    return pl.pallas_call(
        functools.partial(_merge_kernel, tm=tm, sub=sub, tiles_per_seq=seq // tm),
        grid=(n // tm,),
        in_specs=[row(D_MODEL), row(GATE_COLS), row(CONV_COLS), halo, row(CONV_WIDTH), row(SB_WIDTH),
                  _const_spec((8, CONV_WIDTH)), _const_spec((CONV_WIDTH, D_MODEL)),
                  _const_spec((MLA_HEADS * MLA_V, D_MODEL)), _const_spec((SB_WIDTH, D_MODEL)),
                  _const_spec((D_MODEL, D_MODEL)), _const_spec((1, D_MODEL)),
                  _const_spec((D_MODEL, 2 * ROUTER_COLS)),
                  _const_spec((1, ROUTER_COLS)), _const_spec((sub, sub))],
        out_specs=[row(D_MODEL), pl.BlockSpec((tm * ROW_TILES, LANES), lambda i: (i, 0)), row(ROUTER_COLS),
                   _const_spec((1, ROUTER_COLS))],
        out_shape=[jax.ShapeDtypeStruct((n, D_MODEL), F32), jax.ShapeDtypeStruct((n * ROW_TILES, LANES), F32),
                   jax.ShapeDtypeStruct((n, ROUTER_COLS), F32), jax.ShapeDtypeStruct((1, ROUTER_COLS), F32)],
        compiler_params=_params(1),
        name="merge",
    )(x2, gates, conv, conv, o_mla, o_sb, conv_w, wa, wb, wc, wo, ffn_norm, wr_cat, b_router, lower)


def _pack_router(w_rg, b_rg, w_re, b_re):
    pad = ROUTER_COLS - N_EXPERTS - N_GROUPS
    w = jnp.concatenate([w_re, w_rg, jnp.zeros((D_MODEL, pad), F32)], axis=1)
    b = jnp.concatenate([b_re, b_rg, jnp.zeros((pad,), F32)])[None, :]
    hi = w.astype(BF16)
    lo = (w - hi.astype(F32)).astype(BF16)
    return jnp.concatenate([hi, lo], axis=1), b


def _moe_plan(route, counts, n):
    cnt = counts[0, :N_EXPERTS].astype(jnp.int32)
    padded = (cnt + MOE_TILE - 1) // MOE_TILE * MOE_TILE
    ends = jnp.cumsum(padded)
    starts = ends - padded
    n_used = ends[-1] // MOE_TILE
    expert = route[:, 2:4].astype(jnp.int32)
    rank = route[:, 4:6].astype(jnp.int32)
    chosen = expert[..., None] == jnp.arange(N_EXPERTS, dtype=jnp.int32)
    pos = jnp.sum(jnp.where(chosen, starts, 0), axis=-1) + rank
    n_tiles = (2 * n) // MOE_TILE + N_EXPERTS
    tile_start = jnp.minimum(jnp.arange(n_tiles, dtype=jnp.int32), n_used - 1) * MOE_TILE
    tile_expert = jnp.sum((ends[None, :] <= tile_start[:, None]).astype(jnp.int32), axis=1)
    last_tile = jnp.where(padded > 0, ends // MOE_TILE - 1, -1)
    tail = n_used + jnp.arange(N_EXPERTS, dtype=jnp.int32)
    zero_tiles = jnp.concatenate([last_tile, jnp.where(tail < n_tiles, tail, -1)])
    return pos, tile_expert, n_used.reshape(1), zero_tiles, n_tiles


def _dispatch_kernel(zt_ref, pos_ref, h_ref, xs_hbm, zero_ref, sem, zsem, *, td):
    t = pl.program_id(0)

    def fill(j):
        return pltpu.make_async_copy(zero_ref, _token_rows(xs_hbm, zt_ref[j] * MOE_TILE, MOE_TILE), zsem)

    @pl.when(t == 0)
    def _():
        zero_ref[...] = jnp.zeros_like(zero_ref)
        for j in range(2 * N_EXPERTS):
            @pl.when(zt_ref[j] >= 0)
            def _():
                fill(j).start()
        for j in range(2 * N_EXPERTS):
            @pl.when(zt_ref[j] >= 0)
            def _():
                fill(j).wait()

    for r in range(td):
        src = h_ref.at[pl.ds(r * ROW_TILES, ROW_TILES)]
        for k in range(2):
            pltpu.make_async_copy(src, _token_rows(xs_hbm, pos_ref[0, 0, 2 * r + k]), sem).start(priority=k)

    for k in range(2):
        pltpu.make_async_copy(h_ref, _token_rows(xs_hbm, 0, td), sem).wait()


def _dispatch(h2, pos3, zero_tiles, n_tiles, td):
    n = h2.shape[0] // ROW_TILES
    n_steps = n // td
    return pl.pallas_call(
        functools.partial(_dispatch_kernel, td=td),
        grid_spec=pltpu.PrefetchScalarGridSpec(
            num_scalar_prefetch=1, grid=(n_steps,),
            in_specs=[pl.BlockSpec((1, 1, 2 * td), lambda t, zt: (t, 0, 0), memory_space=pltpu.SMEM),
                      pl.BlockSpec((td * ROW_TILES, LANES), lambda t, zt: (t, 0))],
            out_specs=pl.BlockSpec(memory_space=pl.ANY),
            scratch_shapes=[pltpu.VMEM((MOE_TILE * ROW_TILES, LANES), F32), pltpu.SemaphoreType.DMA,
                            pltpu.SemaphoreType.DMA]),
        out_shape=jax.ShapeDtypeStruct((n_tiles * MOE_TILE * ROW_TILES, LANES), F32),
        compiler_params=_params(1),
        name="moe_dispatch",
    )(zero_tiles, pos3, h2)


def _experts_kernel(te_ref, nu_ref, xs_ref, wg_ref, wu_ref, wd_ref, y_ref, wg_bf, wu_bf, wd_bf):
    t = pl.program_id(0)

    @pl.when(t >= nu_ref[0])
    def _():
        y_ref[...] = jnp.zeros_like(y_ref)

    @pl.when((t == 0) | (te_ref[t] != te_ref[jnp.maximum(t - 1, 0)]))
    def _():
        wg_bf[...] = wg_ref[0].astype(BF16)
        wu_bf[...] = wu_ref[0].astype(BF16)
        wd_bf[...] = wd_ref[0].astype(BF16)

    @pl.when(t < nu_ref[0])
    def _():
        x = jnp.concatenate([xs_ref[pl.ds(c, MOE_TILE, stride=ROW_TILES), :] for c in range(ROW_TILES)],
                            axis=1).astype(BF16)
        g = jnp.dot(x, wg_bf[...], preferred_element_type=F32)
        u = jnp.dot(x, wu_bf[...], preferred_element_type=F32)
        act = (g * jax.nn.sigmoid(g) * u).astype(BF16)
        y = jnp.dot(act, wd_bf[...], preferred_element_type=F32)
        for c in range(ROW_TILES):
            y_ref[pl.ds(c, MOE_TILE, stride=ROW_TILES), :] = y[:, c * LANES:(c + 1) * LANES]


def _experts(xs, tile_expert, n_used, wg, wu, wd, layer, n_tiles):
    shape = (MOE_TILE * ROW_TILES, LANES)
    rows_in = pl.BlockSpec(shape, lambda t, te, nu: (jnp.minimum(t, nu[0] - 1), 0))
    rows_out = pl.BlockSpec(shape, lambda t, te, nu: (t, 0))
    weight = lambda shape: pl.BlockSpec((1,) + shape, lambda t, te, nu: (layer * N_EXPERTS + te[t], 0, 0))
    up, down = (D_MODEL, D_EXPERT), (D_EXPERT, D_MODEL)
    return pl.pallas_call(
        _experts_kernel,
        grid_spec=pltpu.PrefetchScalarGridSpec(
            num_scalar_prefetch=2, grid=(n_tiles,),
            in_specs=[rows_in, weight(up), weight(up), weight(down)],
            out_specs=rows_out,
            scratch_shapes=[pltpu.VMEM(up, BF16), pltpu.VMEM(up, BF16), pltpu.VMEM(down, BF16)]),
        out_shape=jax.ShapeDtypeStruct(xs.shape, F32),
        compiler_params=_params(1),
        name="moe_experts",
    )(tile_expert, n_used, xs, wg, wu, wd)


def _combine_norm_kernel(pos_ref, posn_ref, x_ref, route_ref, y_hbm, fin_ref, o_ref, ybuf, sem, *, tc, n_steps):
    def consume(x, issue_part):
        o_ref[...] = _rms(x, fin_ref[...])

    _combine_steps(pos_ref, posn_ref, x_ref, route_ref, y_hbm, ybuf, sem, tc=tc, n_steps=n_steps, consume=consume)


def _combine_norm(x_mid, route, y, pos, final_norm, tc):
    n = x_mid.shape[0]
    n_steps = n // tc
    in_specs, scratch = _combine_specs(tc, n_steps)
    pos3 = pos.reshape(n_steps, 1, 2 * tc)
    return pl.pallas_call(
        functools.partial(_combine_norm_kernel, tc=tc, n_steps=n_steps),
        grid=(n_steps,),
        in_specs=in_specs + [_const_spec((1, D_MODEL))],
        out_specs=pl.BlockSpec((tc, D_MODEL), lambda t: (t, 0)),
        out_shape=jax.ShapeDtypeStruct((n, D_MODEL), F32),
        scratch_shapes=scratch,
        compiler_params=_params(1),
        name="moe_combine_norm",
    )(pos3, pos3, x_mid, route, y, final_norm)


def _moe_experts(h2, route, counts, wg, wu, wd, layer):
    n = route.shape[0]
    pos, tile_expert, n_used, zero_tiles, n_tiles = _moe_plan(route, counts, n)
    td = _tile(n, 512)
    xs = _dispatch(h2, pos.reshape(n // td, 1, 2 * td), zero_tiles, n_tiles, td)
    return _experts(xs, tile_expert, n_used, wg, wu, wd, layer, n_tiles), pos


def _tile(n, pref):
    t = min(n, pref)
    assert n % t == 0, (n, t)
    return t


def kernel(x, positions, attn_norm, w_in, b_gate, conv_w, w_out_conv, q_norm, kv_norm, w_uq, w_ukv, w_out_mla, w_out_sb, w_o, ffn_norm, w_router_group, b_router_group, w_router_expert, b_router_expert, w_exp_gate, w_exp_up, w_exp_down, final_norm):
    batch, seq, d = x.shape
    assert d == D_MODEL
    n = batch * seq
    depth = w_in.shape[0]
    tm_rows = _tile(seq, 512)
    tq = _tile(seq, 256)
    x2 = x.reshape(n, d)
    cos, sin = _rope_tables(positions, _tile(n, 1024))
    wg_all = w_exp_gate.reshape(depth * N_EXPERTS, D_MODEL, D_EXPERT)
    wu_all = w_exp_up.reshape(depth * N_EXPERTS, D_MODEL, D_EXPERT)
    wd_all = w_exp_down.reshape(depth * N_EXPERTS, D_EXPERT, D_MODEL)
    w_in_packed = _pack_w_in(w_in)
    moe = None
    for l in range(depth):
        proj_args = (attn_norm[l][None, :], w_in_packed, l, b_gate[l][None, :], tm_rows)
        if moe is None:
            gates, conv, lat, sb = _in_proj(x2, *proj_args)
        else:
            x2, gates, conv, lat, sb = _combine_in_proj(*moe, *proj_args)
        wq_main, wq_swap, wk, wv = _pack_mla_weights(w_uq[l], w_ukv[l])
        q, k, v = _mla_prep(lat, cos, sin, q_norm[l][None, :], kv_norm[l][None, :], wq_main, wq_swap, wk, wv,
                            tm_rows)
        o_mla = _mla_attn(q, k, v, batch, seq, tq)
        o_sb = _sb_attn(sb, batch, seq, tq)
        wr_cat, b_router = _pack_router(w_router_group[l], b_router_group[l], w_router_expert[l],
                                        b_router_expert[l])
        conv_w8 = jnp.concatenate([conv_w[l], jnp.zeros((8 - CONV_K, CONV_WIDTH), F32)], axis=0)
        x_mid, h2, route, counts = _merge(x2, gates, conv, o_mla, o_sb, conv_w8, w_out_conv[l].astype(BF16),
                                          w_out_mla[l].astype(BF16), w_out_sb[l].astype(BF16),
                                          w_o[l].astype(BF16), ffn_norm[l][None, :], wr_cat, b_router, seq,
                                          tm_rows)
        y, pos = _moe_experts(h2, route, counts, wg_all, wu_all, wd_all, l)
        moe = (x_mid, route, y, pos)
    out = _combine_norm(*moe, final_norm[None, :], _tile(n, 512))
    return out.reshape(batch, seq, d)
```
